```python
import math
import jax
import jax.numpy as jnp
from jax import lax
import numpy as np

D_MODEL = 2048
BATCH = 2
SEQ = 8192
DEPTH = 2

GRID_W = 64
CTX_LEN = 256
N_MIXERS = 4
D_MIX = D_MODEL
W_GROUP = D_MIX // N_MIXERS
N_PROJ = 9
P_IN = N_PROJ * W_GROUP

RG_BLOCKS = 4
RG_BW = W_GROUP // RG_BLOCKS
RG_CONV = 4
RG_PAD = (2, 1)
RG_C = 8.0

SC_CONV = 3
SC_PAD = (1, 1)

POOL_WINDOWS = (2, 4, 8, 16)
POOL_GW = W_GROUP // len(POOL_WINDOWS)

DA_HEADS = 4
DA_HD = W_GROUP // (2 * DA_HEADS)
DA_VD = 2 * DA_HD
ROPE_THETA = 10000.0
Q_BLOCK = 128

N_GROUPS = 4
EXP_PER_GROUP = 8
N_EXPERTS = N_GROUPS * EXP_PER_GROUP
D_EXPERT = D_MODEL // 4
TOP_K = 2

N_MOD = 6
EPS = 1e-6

kernel_name = 'hymba_style_flow_backbone'


def rmsnorm(x, g):
    xf = x.astype(jnp.float32)
    r = lax.rsqrt(jnp.mean(xf * xf, axis=-1, keepdims=True) + EPS)
    return (xf * r).astype(x.dtype) * g


def dwconv(x, w, pad):
    L = x.shape[1]
    xp = jnp.pad(x, ((0, 0), pad, (0, 0)))
    return sum(w[k] * xp[:, k:k + L] for k in range(w.shape[0]))


def _lin_comb(e1, e2):
    a1, b1 = e1
    a2, b2 = e2
    return a1 * a2, a2 * b1 + b2


def rglru_scan(u, w_gate, b_gate, lam, h0, reverse):
    B_, L, _ = u.shape
    ub = u.reshape(B_, L, RG_BLOCKS, RG_BW).astype(jnp.float32)
    g = jnp.einsum('blnd,gnde->gblne', ub, w_gate.astype(jnp.float32)) + b_gate.astype(jnp.float32)[:, None, None]
    r = jax.nn.sigmoid(g[0])
    i = jax.nn.sigmoid(g[1])
    log_a = -RG_C * r * jax.nn.softplus(-lam.astype(jnp.float32).reshape(RG_BLOCKS, RG_BW))
    a = jnp.exp(log_a)
    bx = jnp.sqrt(-jnp.expm1(2.0 * log_a)) * (i * ub)
    if h0 is not None:
        edge = -1 if reverse else 0
        bx = bx.at[:, edge].add(a[:, edge] * h0)
    _, h = lax.associative_scan(_lin_comb, (a, bx), axis=1, reverse=reverse)
    return h


def pool_mixer(x, w, b, scale):
    B_, L, _ = x.shape
    xf = x.astype(jnp.float32)
    cs = jnp.pad(jnp.cumsum(xf, axis=1), ((0, 0), (1, 0), (0, 0)))
    t = np.arange(L)
    outs = []
    for gi, win in enumerate(POOL_WINDOWS):
        lo = np.clip(t - win // 2, 0, L)
        hi = np.clip(t + win // 2, 0, L)
        cnt = jnp.asarray((hi - lo).astype(np.float32))[None, :, None]
        sl = slice(gi * POOL_GW, (gi + 1) * POOL_GW)
        csg = cs[..., sl]
        outs.append((csg[:, hi] - csg[:, lo]) / cnt - xf[..., sl])
    d = jnp.stack(outs, axis=2)
    y = jnp.einsum('blgd,gde->blge', d, w) + b
    return (y.reshape(B_, L, W_GROUP) * scale).astype(x.dtype)


def rope_axis(x, pos):
    nf = x.shape[-1] // 2
    inv = ROPE_THETA ** (-jnp.arange(nf, dtype=jnp.float32) / nf)
    ang = pos.astype(jnp.float32)[:, None] * inv
    cos = jnp.cos(ang)[None, :, None, None, :]
    sin = jnp.sin(ang)[None, :, None, None, :]
    xf = x.astype(jnp.float32)
    x1, x2 = xf[..., :nf], xf[..., nf:]
    return jnp.concatenate([x1 * cos - x2 * sin, x1 * sin + x2 * cos], axis=-1).astype(x.dtype)


def rope_2d(x, rows, cols):
    half = x.shape[-1] // 2
    return jnp.concatenate([rope_axis(x[..., :half], rows), rope_axis(x[..., half:], cols)], axis=-1)


def diff_lambda(lp, lam_init):
    lp = lp.astype(jnp.float32)
    return jnp.exp(jnp.sum(lp[0] * lp[1])) - jnp.exp(jnp.sum(lp[2] * lp[3])) + lam_init


def diff_attn(q, k, v, lam):
    s = jnp.einsum('bqhcd,bkhcd->bhcqk', q, k).astype(jnp.float32) * (DA_HD ** -0.5)
    p = jax.nn.softmax(s, axis=-1)
    pd = p[:, :, 0] - lam * p[:, :, 1]
    return jnp.einsum('bhqk,bkhe->bqhe', pd.astype(v.dtype), v)


def diff_attn_latent(q, k_all, v_all, lam):
    B_, L = q.shape[:2]
    nb = L // Q_BLOCK
    qb = jnp.swapaxes(q.reshape(B_, nb, Q_BLOCK, DA_HEADS, 2, DA_HD), 0, 1)
    o = lax.map(lambda qi: diff_attn(qi, k_all, v_all, lam), qb)
    return jnp.swapaxes(o, 0, 1).reshape(B_, L, DA_HEADS, DA_VD)


def head_group_concat(a_gate, h_rec, b_b, b_c, b_x, c_x, y_att, sc_w, pool_w, pool_b, pool_scale):
    ya = jax.nn.gelu(a_gate) * h_rec.reshape(a_gate.shape).astype(a_gate.dtype)
    yb = b_b * dwconv(b_c * b_x, sc_w, SC_PAD)
    yc = pool_mixer(c_x, pool_w, pool_b, pool_scale)
    return jnp.concatenate([ya, yb, yc, y_att.reshape(a_gate.shape)], axis=-1)


def hier_moe(t, w_grp, b_grp, w_exp, b_exp, w_up, w_down):
    n_tok = t.shape[0]
    g_logit = (t @ w_grp).astype(jnp.float32) + b_grp
    g_prob = jax.nn.softmax(g_logit, axis=-1)
    g_idx = jnp.argmax(g_logit, axis=-1)
    g_w = jnp.take_along_axis(g_prob, g_idx[:, None], axis=-1)
    e_logit = ((t @ w_exp).astype(jnp.float32) + b_exp).reshape(n_tok, N_GROUPS, EXP_PER_GROUP)
    e_logit = jnp.take_along_axis(e_logit, g_idx[:, None, None], axis=1)[:, 0]
    top_v, top_i = lax.top_k(e_logit, TOP_K)
    top_w = jax.nn.softmax(top_v, axis=-1) * g_w
    e_sel = g_idx[:, None] * EXP_PER_GROUP + top_i
    gate = jnp.einsum('tk,tke->te', top_w, jax.nn.one_hot(e_sel, N_EXPERTS, dtype=jnp.float32)).astype(t.dtype)
    y = jnp.zeros_like(t)
    for e in range(N_EXPERTS):
        hu = t @ w_up[e]
        h = jax.nn.silu(hu[:, :D_EXPERT]) * hu[:, D_EXPERT:]
        y = y + gate[:, e:e + 1] * (h @ w_down[e])
    return y


def setup_inputs(seed: int = 0) -> dict:
    key = jax.random.key(seed)
    ks = jax.random.split(key, 32)
    f32 = jnp.float32

    def nrm(k, shape, scale):
        return jax.random.normal(k, shape, f32) * scale

    D, W = D_MODEL, W_GROUP
    a_target = jax.random.uniform(ks[12], (DEPTH, 2, W), f32, 0.9, 0.999)
    s = a_target ** (1.0 / RG_C)
    rg_lam = jnp.log(s) - jnp.log1p(-s)
    return {
        'x': nrm(ks[0], (BATCH, SEQ, D), 1.0),
        'c': nrm(ks[1], (BATCH, D), 1.0),
        'ctx': nrm(ks[2], (BATCH, CTX_LEN, D), 1.0),
        'c_ctx': nrm(ks[3], (D,), 1.0),
        'mod_w': nrm(ks[4], (DEPTH, D, N_MOD * D), 0.5 * D ** -0.5),
        'mod_b': nrm(ks[5], (DEPTH, N_MOD * D), 0.1),
        'norm_mix': 1.0 + nrm(ks[6], (DEPTH, D), 0.1),
        'norm_ffn': 1.0 + nrm(ks[7], (DEPTH, D), 0.1),
        'w_in': nrm(ks[8], (DEPTH, D, P_IN), D ** -0.5),
        'w_out': nrm(ks[9], (DEPTH, D_MIX, D), D_MIX ** -0.5),
        'rg_conv_w': nrm(ks[10], (DEPTH, RG_CONV, W), RG_CONV ** -0.5),
        'rg_conv_b': nrm(ks[11], (DEPTH, W), 0.02),
        'rg_gate_w': nrm(ks[13], (DEPTH, 2, 2, RG_BLOCKS, RG_BW, RG_BW), RG_BW ** -0.5),
        'rg_gate_b': nrm(ks[14], (DEPTH, 2, 2, RG_BLOCKS, RG_BW), 0.02),
        'rg_lam': rg_lam,
        'sc_conv_w': nrm(ks[15], (DEPTH, SC_CONV, W), SC_CONV ** -0.5),
        'pool_w': nrm(ks[16], (DEPTH, len(POOL_WINDOWS), POOL_GW, POOL_GW), POOL_GW ** -0.5),
        'pool_b': nrm(ks[17], (DEPTH, len(POOL_WINDOWS), POOL_GW), 0.02),
        'pool_scale': 1.0 + nrm(ks[18], (DEPTH, W), 0.1),
        'da_lam': nrm(ks[19], (DEPTH, 4, DA_HD), 0.1),
        'da_norm': 1.0 + nrm(ks[20], (DEPTH, DA_VD), 0.1),
        'moe_grp_w': nrm(ks[21], (DEPTH, D, N_GROUPS), D ** -0.5),
        'moe_grp_b': nrm(ks[22], (DEPTH, N_GROUPS), 0.01),
        'moe_exp_w': nrm(ks[23], (DEPTH, D, N_EXPERTS), D ** -0.5),
        'moe_exp_b': nrm(ks[24], (DEPTH, N_EXPERTS), 0.01),
        'moe_up': nrm(ks[25], (DEPTH, N_EXPERTS, D, 2 * D_EXPERT), D ** -0.5),
        'moe_down': nrm(ks[26], (DEPTH, N_EXPERTS, D_EXPERT, D), D_EXPERT ** -0.5),
        'norm_final': 1.0 + nrm(ks[27], (D,), 0.1),
    }


def reference(x, c, ctx, c_ctx, mod_w, mod_b, norm_mix, norm_ffn, w_in, w_out,
              rg_conv_w, rg_conv_b, rg_gate_w, rg_gate_b, rg_lam, sc_conv_w,
              pool_w, pool_b, pool_scale, da_lam, da_norm,
              moe_grp_w, moe_grp_b, moe_exp_w, moe_exp_b, moe_up, moe_down, norm_final):
    B_, L, D = x.shape
    n_ctx = ctx.shape[1]
    n_rows = L // GRID_W
    rows = jnp.repeat(jnp.arange(n_rows, dtype=jnp.int32), GRID_W)
    cols = jnp.tile(jnp.arange(GRID_W, dtype=jnp.int32), n_rows)
    s_lat = jax.nn.silu(c)
    s_ctx = jax.nn.silu(c_ctx)
    xl, xc = x, ctx
    for l in range(DEPTH):
        last = l == DEPTH - 1
        lam_init = 0.8 - 0.6 * math.exp(-0.3 * l)
        m_l = jnp.split((s_lat @ mod_w[l] + mod_b[l])[:, None, :], N_MOD, axis=-1)
        m_c = jnp.split(s_ctx @ mod_w[l] + mod_b[l], N_MOD, axis=-1)

        hl = rmsnorm(xl, norm_mix[l]) * (1.0 + m_l[1]) + m_l[0]
        hc = rmsnorm(xc, norm_mix[l]) * (1.0 + m_c[1]) + m_c[0]
        al_x, al_g, bl_b, bl_c, bl_x, cl_x, ql, kl, vl = jnp.split(hl @ w_in[l], N_PROJ, axis=-1)
        ac_x, ac_g, bc_b, bc_c, bc_x, cc_x, qc, kc, vc = jnp.split(hc @ w_in[l], N_PROJ, axis=-1)

        ul = dwconv(al_x, rg_conv_w[l], RG_PAD) + rg_conv_b[l]
        uc = dwconv(ac_x, rg_conv_w[l], RG_PAD) + rg_conv_b[l]
        hcf = rglru_scan(uc, rg_gate_w[l, 0], rg_gate_b[l, 0], rg_lam[l, 0], None, False)
        hcb = rglru_scan(uc, rg_gate_w[l, 1], rg_gate_b[l, 1], rg_lam[l, 1], None, True)
        hlf = rglru_scan(ul, rg_gate_w[l, 0], rg_gate_b[l, 0], rg_lam[l, 0], hcf[:, -1], False)
        hlb = rglru_scan(ul, rg_gate_w[l, 1], rg_gate_b[l, 1], rg_lam[l, 1], hcb[:, 0], True)

        lam = diff_lambda(da_lam[l], lam_init)
        kc = kc.reshape(B_, n_ctx, DA_HEADS, 2, DA_HD)
        vc = vc.reshape(B_, n_ctx, DA_HEADS, DA_VD)
        ql = rope_2d(ql.reshape(B_, L, DA_HEADS, 2, DA_HD), rows, cols)
        kl = rope_2d(kl.reshape(B_, L, DA_HEADS, 2, DA_HD), rows, cols)
        k_all = jnp.concatenate([kc, kl], axis=1)
        v_all = jnp.concatenate([vc, vl.reshape(B_, L, DA_HEADS, DA_VD)], axis=1)
        ydl = rmsnorm(diff_attn_latent(ql, k_all, v_all, lam), da_norm[l]) * (1.0 - lam_init)

        yl = head_group_concat(al_g, hlf + hlb, bl_b, bl_c, bl_x, cl_x, ydl,
                               sc_conv_w[l], pool_w[l], pool_b[l], pool_scale[l])
        xl = xl + m_l[2] * (yl @ w_out[l])
        if not last:
            ydc = rmsnorm(diff_attn(qc.reshape(B_, n_ctx, DA_HEADS, 2, DA_HD), kc, vc, lam),
                          da_norm[l]) * (1.0 - lam_init)
            yc = head_group_concat(ac_g, hcf + hcb, bc_b, bc_c, bc_x, cc_x, ydc,
                                   sc_conv_w[l], pool_w[l], pool_b[l], pool_scale[l])
            xc = xc + m_c[2] * (yc @ w_out[l])

        fl = rmsnorm(xl, norm_ffn[l]) * (1.0 + m_l[4]) + m_l[3]
        if last:
            y_l = hier_moe(fl.reshape(-1, D), moe_grp_w[l], moe_grp_b[l], moe_exp_w[l], moe_exp_b[l],
                           moe_up[l], moe_down[l])
            xl = xl + m_l[5] * y_l.reshape(xl.shape)
        else:
            fc = rmsnorm(xc, norm_ffn[l]) * (1.0 + m_c[4]) + m_c[3]
            tok = jnp.concatenate([fc.reshape(-1, D), fl.reshape(-1, D)], axis=0)
            y = hier_moe(tok, moe_grp_w[l], moe_grp_b[l], moe_exp_w[l], moe_exp_b[l],
                         moe_up[l], moe_down[l])
            xc = xc + m_c[5] * y[:B_ * n_ctx].reshape(xc.shape)
            xl = xl + m_l[5] * y[B_ * n_ctx:].reshape(xl.shape)
    return rmsnorm(xl, norm_final)
```

```python
import functools
import math

import jax
import jax.numpy as jnp
from jax import lax
from jax.experimental import pallas as pl
from jax.experimental.pallas import tpu as pltpu

F32 = jnp.float32
BF16 = jnp.bfloat16

W_GROUP = 512
N_PROJ = 9
RG_BLOCKS = 4
RG_BW = 128
RG_C = 8.0
POOL_WINDOWS = (2, 4, 8, 16)
POOL_GW = 128
DA_HEADS = 4
DA_HD = 64
DA_VD = 128
ROPE_THETA = 10000.0
GRID_W = 64
N_GROUPS = 4
EXP_PER_GROUP = 8
N_EXPERTS = 32
D_EXPERT = 512
N_MOD = 6
EPS = 1e-6

SUBLANES = 8
LANES = 128
ROW_TILE = 256
MM_TILE = 512
KV_TILE = 512
EXP_TILE = 256
VMEM_LIMIT = 56 * 1024 * 1024


def _cparams(n_axes):
    return pltpu.CompilerParams(dimension_semantics=("arbitrary",) * n_axes,
                                vmem_limit_bytes=VMEM_LIMIT)


def _mod_kernel(c_ref, w_ref, b_ref, o_ref):
    c = c_ref[...]
    s = c * jax.nn.sigmoid(c)
    o_ref[0] = jnp.dot(s.astype(BF16), w_ref[0].astype(BF16), preferred_element_type=F32) + b_ref[0]


def _modulation(cc, mod_w, mod_b):
    depth, d, n = mod_w.shape
    tn = 1024
    out = pl.pallas_call(
        _mod_kernel,
        grid=(depth, n // tn),
        in_specs=[pl.BlockSpec((SUBLANES, d), lambda l, j: (0, 0)),
                  pl.BlockSpec((1, d, tn), lambda l, j: (l, 0, j)),
                  pl.BlockSpec((1, 1, tn), lambda l, j: (l, 0, j))],
        out_specs=pl.BlockSpec((1, SUBLANES, tn), lambda l, j: (l, 0, j)),
        out_shape=jax.ShapeDtypeStruct((depth, SUBLANES, n), F32),
        compiler_params=_cparams(2),
        name="modulation",
    )(cc, mod_w, mod_b.reshape(depth, 1, n))
    return out.reshape(depth, SUBLANES, N_MOD, d)


def _rms_mod(x, g, shift, scale):
    r = lax.rsqrt(jnp.mean(x * x, axis=-1, keepdims=True) + EPS)
    return (x * r) * g * (1.0 + scale) + shift


class _Layout:
    def __init__(self, n_batch, n_ctx, seq):
        assert n_ctx == ROW_TILE, "one sequence tile per context sequence"
        assert n_batch * n_ctx == MM_TILE, "context rows fill exactly one projection tile"
        assert seq % MM_TILE == 0 and seq % KV_TILE == 0 and seq % GRID_W == 0
        self.b = n_batch
        self.n_ctx = n_ctx
        self.seq = seq
        self.rows = n_batch * (n_ctx + seq)
        self.nl = seq // ROW_TILE
        self.n_tiles = self.rows // ROW_TILE

    def mod_row_of_seq_tile(self, i):
        return jnp.where(i < self.b, 0, 1 + (i - self.b) // self.nl)

    def mod_row_of_mm_tile(self, i):
        return jnp.where(i < 1, 0, 1 + (i - 1) // (self.seq // MM_TILE))


def _inproj_kernel(x_ref, g_ref, m_ref, w_ref, o_ref, h_scr):
    @pl.when(pl.program_id(1) == 0)
    def _():
        h = _rms_mod(x_ref[...], g_ref[...], m_ref[0:1, :], m_ref[1:2, :])
        h_scr[...] = h.astype(BF16)

    o_ref[...] = jnp.dot(h_scr[...], w_ref[...], preferred_element_type=F32)


def _in_projection(lay, x_all, gain, mods, w_bf):
    t, d = x_all.shape
    n = w_bf.shape[1]
    tn = 1536
    return pl.pallas_call(
        _inproj_kernel,
        grid=(t // MM_TILE, n // tn),
        in_specs=[pl.BlockSpec((MM_TILE, d), lambda i, j: (i, 0)),
                  pl.BlockSpec((1, d), lambda i, j: (0, 0)),
                  pl.BlockSpec((None, N_MOD, d), lambda i, j: (lay.mod_row_of_mm_tile(i), 0, 0)),
                  pl.BlockSpec((d, tn), lambda i, j: (0, j))],
        out_specs=pl.BlockSpec((MM_TILE, tn), lambda i, j: (i, j)),
        out_shape=jax.ShapeDtypeStruct((t, n), F32),
        scratch_shapes=[pltpu.VMEM((MM_TILE, d), BF16)],
        compiler_params=_cparams(2),
        name="in_projection",
    )(x_all, gain.reshape(1, d), mods, w_bf)


def _fill_ext(ext, prev_ref, x, next_ref, is_first, is_last):
    ext[0:SUBLANES, :] = jnp.where(is_first, 0.0, prev_ref[...])
    ext[SUBLANES:SUBLANES + ROW_TILE, :] = x
    ext[SUBLANES + ROW_TILE:, :] = jnp.where(is_last, 0.0, next_ref[...])


def _shifted(ext, off, cols=slice(None)):
    return ext[SUBLANES + off:SUBLANES + off + ROW_TILE, cols]


def _rglru_kernel(reverse, n_lat_tiles, *refs):
    if reverse:
        (x_ref, xp_ref, xn_ref, cw_ref, cb_ref, gw_ref, gb_ref, lam_ref, hf_ref, ag_ref,
         o_ref, ext, a_scr, b_scr, carry) = refs
    else:
        (x_ref, xp_ref, xn_ref, cw_ref, cb_ref, gw_ref, gb_ref, lam_ref,
         o_ref, ext, a_scr, b_scr, carry) = refs
    s = pl.program_id(1)
    if reverse:
        is_first = (s == 0) | (s == n_lat_tiles)
        is_last = (s == 0) | (s == 1)
    else:
        is_first = (s == 0) | (s == 1)
        is_last = (s == 0) | (s == n_lat_tiles)

    @pl.when(s == 0)
    def _():
        carry[...] = jnp.zeros_like(carry)

    _fill_ext(ext, xp_ref, x_ref[...], xn_ref, is_first, is_last)
    u = cb_ref[...] + sum(cw_ref[k:k + 1, :] * _shifted(ext, k - 2) for k in range(4))

    sp = -lam_ref[...]
    sp = jnp.maximum(sp, 0.0) + jnp.log1p(jnp.exp(-jnp.abs(sp)))
    for n in range(RG_BLOCKS):
        cols = slice(n * RG_BW, (n + 1) * RG_BW)
        un = u[:, cols]
        g = jnp.dot(un.astype(BF16), gw_ref[n], preferred_element_type=F32)
        r = jax.nn.sigmoid(g[:, :RG_BW] + gb_ref[0:1, cols])
        gi = jax.nn.sigmoid(g[:, RG_BW:] + gb_ref[1:2, cols])
        log_a = -RG_C * r * sp[:, cols]
        a_scr[:, cols] = jnp.exp(log_a)
        th = jnp.tanh(log_a)
        b_scr[:, cols] = jnp.sqrt(-2.0 * th / (1.0 - th)) * (gi * un)

    row = lax.broadcasted_iota(jnp.int32, (SUBLANES, W_GROUP), 0)
    n_groups = ROW_TILE // SUBLANES

    def group(gidx, h_prev):
        g0 = (n_groups - 1 - gidx) if reverse else gidx
        start = pl.multiple_of(g0 * SUBLANES, SUBLANES)
        a8 = a_scr[pl.ds(start, SUBLANES), :]
        b8 = b_scr[pl.ds(start, SUBLANES), :]
        for sh in (1, 2, 4):
            rot = (SUBLANES - sh) if reverse else sh
            a_sh = pltpu.roll(a8, rot, 0)
            b_sh = pltpu.roll(b8, rot, 0)
            valid = (row < SUBLANES - sh) if reverse else (row >= sh)
            b8 = jnp.where(valid, a8 * b_sh + b8, b8)
            a8 = jnp.where(valid, a8 * a_sh, a8)
        h8 = a8 * h_prev + b8
        a_scr[pl.ds(start, SUBLANES), :] = h8
        edge = 0 if reverse else SUBLANES - 1
        return h8[edge:edge + 1, :]

    carry[...] = lax.fori_loop(0, n_groups, group, carry[...])

    h = a_scr[...]
    if reverse:
        o_ref[...] = (jax.nn.gelu(ag_ref[...]) * (hf_ref[...] + h)).astype(o_ref.dtype)
    else:
        o_ref[...] = h


def _rglru(lay, proj, conv_w, conv_b, gate_w, gate_b, lam, reverse, hf=None):
    t = proj.shape[0]
    nl = lay.nl
    nb8 = ROW_TILE // SUBLANES

    def tile(b, s):
        lat = (nl - s) if reverse else (s - 1)
        return jnp.where(s == 0, b, lay.b + b * nl + lat)

    main = lambda c: pl.BlockSpec((ROW_TILE, W_GROUP), lambda b, s: (tile(b, s), c))
    prev = pl.BlockSpec((SUBLANES, W_GROUP), lambda b, s: (jnp.maximum(tile(b, s) * nb8 - 1, 0), 0))
    nxt = pl.BlockSpec((SUBLANES, W_GROUP),
                       lambda b, s: (jnp.minimum((tile(b, s) + 1) * nb8, t // SUBLANES - 1), 0))
    const = lambda shape: pl.BlockSpec(shape, lambda b, s: (0,) * len(shape))
    gw = jnp.transpose(gate_w, (1, 2, 0, 3)).reshape(RG_BLOCKS, RG_BW, 2 * RG_BW).astype(BF16)
    gb = gate_b.reshape(2, W_GROUP)
    in_specs = [main(0), prev, nxt, const((4, W_GROUP)), const((1, W_GROUP)),
                const((RG_BLOCKS, RG_BW, 2 * RG_BW)), const((2, W_GROUP)), const((1, W_GROUP))]
    args = [proj, proj, proj, conv_w, conv_b.reshape(1, W_GROUP), gw, gb, lam.reshape(1, W_GROUP)]
    if reverse:
        in_specs += [pl.BlockSpec((ROW_TILE, W_GROUP), lambda b, s: (tile(b, s), 0)), main(1)]
        args += [hf, proj]
    return pl.pallas_call(
        functools.partial(_rglru_kernel, reverse, nl),
        grid=(lay.b, 1 + nl),
        in_specs=in_specs,
        out_specs=pl.BlockSpec((ROW_TILE, W_GROUP), lambda b, s: (tile(b, s), 0)),
        out_shape=jax.ShapeDtypeStruct((t, W_GROUP), BF16 if reverse else F32),
        scratch_shapes=[pltpu.VMEM((ROW_TILE + 2 * SUBLANES, W_GROUP), F32),
                        pltpu.VMEM((ROW_TILE, W_GROUP), F32),
                        pltpu.VMEM((ROW_TILE, W_GROUP), F32),
                        pltpu.VMEM((1, W_GROUP), F32)],
        compiler_params=_cparams(2),
        name="rglru_bwd" if reverse else "rglru_fwd",
    )(*args)


def _conv_pool_kernel(lay, bb_ref, bc_ref, bcp_ref, bcn_ref, bx_ref, bxp_ref, bxn_ref,
                      cx_ref, cxp_ref, cxn_ref, scw_ref, pw_ref, pb_ref, ps_ref,
                      yb_ref, yc_ref, ext):
    i = pl.program_id(0)
    j = jnp.maximum(i - lay.b, 0) % lay.nl
    is_ctx = i < lay.b
    is_first = is_ctx | (j == 0)
    is_last = is_ctx | (j == lay.nl - 1)

    ext[0:SUBLANES, :] = jnp.where(is_first, 0.0, bcp_ref[...] * bxp_ref[...])
    ext[SUBLANES:SUBLANES + ROW_TILE, :] = bc_ref[...] * bx_ref[...]
    ext[SUBLANES + ROW_TILE:, :] = jnp.where(is_last, 0.0, bcn_ref[...] * bxn_ref[...])
    conv = sum(scw_ref[k:k + 1, :] * _shifted(ext, k - 1) for k in range(3))
    yb_ref[...] = (bb_ref[...] * conv).astype(yb_ref.dtype)

    x = cx_ref[...]
    _fill_ext(ext, cxp_ref, x, cxn_ref, is_first, is_last)
    seq_len = jnp.where(is_ctx, lay.n_ctx, lay.seq)
    pos = jnp.where(is_ctx, 0, j * ROW_TILE) + lax.broadcasted_iota(jnp.int32, (ROW_TILE, 1), 0)
    for gi, win in enumerate(POOL_WINDOWS):
        cols = slice(gi * POOL_GW, (gi + 1) * POOL_GW)
        half = win // 2
        tot = sum(_shifted(ext, off, cols) for off in range(-half, half))
        cnt = jnp.minimum(pos + half, seq_len) - jnp.maximum(pos - half, 0)
        dlt = tot / cnt.astype(F32) - x[:, cols]
        y = jnp.dot(dlt.astype(BF16), pw_ref[gi].astype(BF16), preferred_element_type=F32)
        yc_ref[:, cols] = ((y + pb_ref[gi:gi + 1, :]) * ps_ref[:, cols]).astype(yc_ref.dtype)


def _conv_pool(lay, proj, sc_w, pool_w, pool_b, pool_scale):
    t = proj.shape[0]
    nb8 = ROW_TILE // SUBLANES
    main = lambda c: pl.BlockSpec((ROW_TILE, W_GROUP), lambda i: (i, c))
    prev = lambda c: pl.BlockSpec((SUBLANES, W_GROUP), lambda i: (jnp.maximum(i * nb8 - 1, 0), c))
    nxt = lambda c: pl.BlockSpec((SUBLANES, W_GROUP),
                                 lambda i: (jnp.minimum((i + 1) * nb8, t // SUBLANES - 1), c))
    const = lambda shape: pl.BlockSpec(shape, lambda i: (0,) * len(shape))
    n_pool = len(POOL_WINDOWS)
    out = jax.ShapeDtypeStruct((t, W_GROUP), BF16)
    return pl.pallas_call(
        functools.partial(_conv_pool_kernel, lay),
        grid=(t // ROW_TILE,),
        in_specs=[main(2), main(3), prev(3), nxt(3), main(4), prev(4), nxt(4),
                  main(5), prev(5), nxt(5), const((3, W_GROUP)),
                  const((n_pool, POOL_GW, POOL_GW)), const((n_pool, POOL_GW)), const((1, W_GROUP))],
        out_specs=[pl.BlockSpec((ROW_TILE, W_GROUP), lambda i: (i, 0))] * 2,
        out_shape=[out, out],
        scratch_shapes=[pltpu.VMEM((ROW_TILE + 2 * SUBLANES, W_GROUP), F32)],
        compiler_params=_cparams(1),
        name="conv_pool",
    )(*([proj] * 10), sc_w, pool_w, pool_b, pool_scale.reshape(1, W_GROUP))


def _rope_tables(lay):
    nf = DA_HD // 4
    inv = ROPE_THETA ** (-jnp.arange(nf, dtype=F32) / nf)
    pos = jnp.arange(lay.seq, dtype=jnp.int32)
    ang_r = (pos // GRID_W).astype(F32)[:, None] * inv
    ang_c = (pos % GRID_W).astype(F32)[:, None] * inv
    cos = jnp.concatenate([jnp.cos(ang_r)] * 2 + [jnp.cos(ang_c)] * 2, axis=-1)
    sin = jnp.concatenate([-jnp.sin(ang_r), jnp.sin(ang_r), -jnp.sin(ang_c), jnp.sin(ang_c)], axis=-1)
    cos = jnp.concatenate([jnp.ones((ROW_TILE, DA_HD), F32), cos], axis=0)
    sin = jnp.concatenate([jnp.zeros((ROW_TILE, DA_HD), F32), sin], axis=0)
    return jnp.tile(cos, (1, 2)), jnp.tile(sin, (1, 2))


def _rope_kernel(q_ref, k_ref, v_ref, cos_ref, sin_ref, qo_ref, ko_ref, vo_ref):
    cos = cos_ref[...]
    sin = sin_ref[...]
    nf = DA_HD // 4
    lane = lax.broadcasted_iota(jnp.int32, (ROW_TILE, LANES), 1)
    first_half = (lane & (2 * nf - 1)) < nf

    def rot(x):
        partner = jnp.where(first_half, pltpu.roll(x, LANES - nf, 1), pltpu.roll(x, nf, 1))
        return x * cos + partner * sin

    for h in range(DA_HEADS):
        cols = slice(h * DA_VD, (h + 1) * DA_VD)
        qo_ref[:, cols] = (rot(q_ref[:, cols]) * (DA_HD ** -0.5)).astype(BF16)
        ko_ref[:, cols] = rot(k_ref[:, cols]).astype(BF16)
    vo_ref[...] = v_ref[...].astype(BF16)


def _rope(lay, proj, cos, sin):
    t = proj.shape[0]
    main = lambda c: pl.BlockSpec((ROW_TILE, W_GROUP), lambda i: (i, c))
    tab = pl.BlockSpec((ROW_TILE, LANES), lambda i: (jnp.where(i < lay.b, 0, 1 + (i - lay.b) % lay.nl), 0))
    out = jax.ShapeDtypeStruct((t, W_GROUP), BF16)
    return pl.pallas_call(
        _rope_kernel,
        grid=(t // ROW_TILE,),
        in_specs=[main(6), main(7), main(8), tab, tab],
        out_specs=[pl.BlockSpec((ROW_TILE, W_GROUP), lambda i: (i, 0))] * 3,
        out_shape=[out, out, out],
        compiler_params=_cparams(1),
        name="rope",
    )(proj, proj, proj, cos, sin)


def _attn_kernel(lam_init, q_ref, kc_ref, vc_ref, kl_ref, vl_ref, lp_ref, g_ref, o_ref,
                 qq, m_scr, l_scr, acc):
    qi = pl.program_id(2)
    ki = pl.program_id(3)
    tq = q_ref.shape[0]

    def attend(k, v):
        s = lax.dot_general(qq[...], k, (((1,), (1,)), ((), ())), preferred_element_type=F32)
        m_prev = m_scr[:, 0:1]
        m_new = jnp.maximum(m_prev, jnp.max(s, axis=-1, keepdims=True))
        alpha = jnp.exp(m_prev - m_new)
        p = jnp.exp(s - m_new)
        l_new = alpha * l_scr[:, 0:1] + jnp.sum(p, axis=-1, keepdims=True)
        acc[...] = alpha * acc[...] + jnp.dot(p.astype(BF16), v, preferred_element_type=F32)
        m_scr[...] = jnp.broadcast_to(m_new, m_scr.shape)
        l_scr[...] = jnp.broadcast_to(l_new, l_scr.shape)

    @pl.when(ki == 0)
    def _():
        q = q_ref[...]
        lane = lax.broadcasted_iota(jnp.int32, q.shape, 1)
        qq[0:tq, :] = jnp.where(lane < DA_HD, q, jnp.zeros_like(q))
        qq[tq:, :] = jnp.where(lane >= DA_HD, q, jnp.zeros_like(q))
        m_scr[...] = jnp.full_like(m_scr, -jnp.inf)
        l_scr[...] = jnp.zeros_like(l_scr)
        acc[...] = jnp.zeros_like(acc)
        attend(kc_ref[...], vc_ref[...])

    @pl.when(qi > 0)
    def _():
        attend(kl_ref[...], vl_ref[...])

    @pl.when(ki == pl.num_programs(3) - 1)
    def _():
        lp = lp_ref[...]
        lam = (jnp.exp(jnp.sum(lp[0:1] * lp[1:2], axis=-1, keepdims=True))
               - jnp.exp(jnp.sum(lp[2:3] * lp[3:4], axis=-1, keepdims=True)) + lam_init)
        o = acc[0:tq, :] / l_scr[0:tq, 0:1] - lam * (acc[tq:, :] / l_scr[tq:, 0:1])
        r = lax.rsqrt(jnp.mean(o * o, axis=-1, keepdims=True) + EPS)
        o_ref[...] = ((o * r) * g_ref[...] * (1.0 - lam_init)).astype(o_ref.dtype)


def _attention(lay, qs, kr, vb, da_lam, da_norm, lam_init):
    t = qs.shape[0]
    nl = lay.nl
    tq = ROW_TILE
    n_kv = lay.seq // KV_TILE
    q_spec = pl.BlockSpec(
        (tq, DA_VD), lambda b, h, qi, ki: (jnp.where(qi == 0, b, lay.b + b * nl + qi - 1), h))
    ctx_spec = pl.BlockSpec((ROW_TILE, DA_VD), lambda b, h, qi, ki: (b, h))
    lat_blk0 = lay.b * lay.n_ctx // KV_TILE
    lat_spec = pl.BlockSpec((KV_TILE, DA_VD), lambda b, h, qi, ki: (lat_blk0 + b * n_kv + ki, h))
    const = lambda shape: pl.BlockSpec(shape, lambda b, h, qi, ki: (0,) * len(shape))
    return pl.pallas_call(
        functools.partial(_attn_kernel, lam_init),
        grid=(lay.b, DA_HEADS, 1 + nl, n_kv),
        in_specs=[q_spec, ctx_spec, ctx_spec, lat_spec, lat_spec, const((4, DA_HD)), const((1, DA_VD))],
        out_specs=q_spec,
        out_shape=jax.ShapeDtypeStruct((t, W_GROUP), BF16),
        scratch_shapes=[pltpu.VMEM((2 * tq, DA_VD), BF16),
                        pltpu.VMEM((2 * tq, LANES), F32),
                        pltpu.VMEM((2 * tq, LANES), F32),
                        pltpu.VMEM((2 * tq, DA_VD), F32)],
        compiler_params=_cparams(4),
        name="attention",
    )(qs, kr, vb, kr, vb, da_lam, da_norm.reshape(1, DA_VD))


def _outproj_kernel(ya_ref, yb_ref, yc_ref, yd_ref, w_ref, x_ref, m_ref, o_ref):
    y = jnp.concatenate([ya_ref[...], yb_ref[...], yc_ref[...], yd_ref[...]], axis=-1)
    o_ref[...] = x_ref[...] + m_ref[2:3, :] * jnp.dot(y, w_ref[...], preferred_element_type=F32)


def _out_projection(lay, ys, w_bf, x_all, mods):
    t, d = x_all.shape
    tn = 1024
    y_spec = pl.BlockSpec((MM_TILE, W_GROUP), lambda j, i: (i, 0))
    return pl.pallas_call(
        _outproj_kernel,
        grid=(d // tn, t // MM_TILE),
        in_specs=[y_spec] * 4 + [
            pl.BlockSpec((d, tn), lambda j, i: (0, j)),
            pl.BlockSpec((MM_TILE, tn), lambda j, i: (i, j)),
            pl.BlockSpec((None, N_MOD, tn), lambda j, i: (lay.mod_row_of_mm_tile(i), 0, j))],
        out_specs=pl.BlockSpec((MM_TILE, tn), lambda j, i: (i, j)),
        out_shape=jax.ShapeDtypeStruct((t, d), F32),
        compiler_params=_cparams(2),
        name="out_projection",
    )(*ys, w_bf, x_all, mods)


R_E1, R_E2, R_W1, R_W2, R_RANK1, R_RANK2 = range(6)


def _router_kernel(x_ref, g_ref, m_ref, w_ref, b_ref, fl_ref, rec_ref, cnt_ref, counts):
    i = pl.program_id(0)

    @pl.when(i == 0)
    def _():
        counts[...] = jnp.zeros_like(counts)

    fl = _rms_mod(x_ref[...], g_ref[...], m_ref[3:4, :], m_ref[4:5, :])
    fl_ref[...] = fl
    logit = jnp.dot(fl, w_ref[...], preferred_element_type=F32, precision=lax.Precision.HIGHEST) + b_ref[...]
    tm = logit.shape[0]
    lane = lax.broadcasted_iota(jnp.int32, (tm, LANES), 1).astype(F32)
    neg = -jnp.inf

    def first_argmax(vals):
        top = jnp.max(vals, axis=-1, keepdims=True)
        return top, jnp.min(jnp.where(vals == top, lane, float(LANES)), axis=-1, keepdims=True)

    is_grp = (lane >= N_EXPERTS) & (lane < N_EXPERTS + N_GROUPS)
    g_top, g_lane = first_argmax(jnp.where(is_grp, logit, neg))
    g_w = 1.0 / jnp.sum(jnp.where(is_grp, jnp.exp(logit - g_top), 0.0), axis=-1, keepdims=True)

    grp_start = (g_lane - N_EXPERTS) * EXP_PER_GROUP
    in_grp = (lane >= grp_start) & (lane < grp_start + EXP_PER_GROUP)
    v1, e1 = first_argmax(jnp.where(in_grp, logit, neg))
    v2, e2 = first_argmax(jnp.where(in_grp & (lane != e1), logit, neg))
    z = jnp.exp(v2 - v1)
    w1 = g_w / (1.0 + z)
    w2 = g_w * z / (1.0 + z)

    oh1 = lane == e1
    oh2 = lane == e2
    oh = jnp.where(oh1 | oh2, 1.0, 0.0).astype(BF16)
    r_i = lax.broadcasted_iota(jnp.int32, (tm, tm), 0)
    c_i = lax.broadcasted_iota(jnp.int32, (tm, tm), 1)
    below = jnp.where(c_i < r_i, 1.0, 0.0).astype(BF16)
    before = counts[0:1, :] + jnp.dot(below, oh, preferred_element_type=F32)
    rank1 = jnp.sum(jnp.where(oh1, before, 0.0), axis=-1, keepdims=True)
    rank2 = jnp.sum(jnp.where(oh2, before, 0.0), axis=-1, keepdims=True)
    counts[...] = counts[...] + jnp.sum(oh.astype(F32), axis=0, keepdims=True)
    cnt_ref[...] = counts[...]

    rec = jnp.zeros((tm, LANES), F32)
    for slot, val in ((R_E1, e1), (R_E2, e2), (R_W1, w1), (R_W2, w2), (R_RANK1, rank1), (R_RANK2, rank2)):
        rec = jnp.where(lane == slot, val, rec)
    rec_ref[...] = rec


def _router(lay, x_all, gain, mods, w_exp, b_exp, w_grp, b_grp, first_tile):
    t, d = x_all.shape
    n_rows = t - first_tile * MM_TILE
    pad = LANES - N_EXPERTS - N_GROUPS
    w = jnp.concatenate([w_exp, w_grp, jnp.zeros((d, pad), F32)], axis=1)
    b = jnp.concatenate([b_exp, b_grp, jnp.zeros((pad,), F32)]).reshape(1, LANES)
    return pl.pallas_call(
        _router_kernel,
        grid=(n_rows // MM_TILE,),
        in_specs=[pl.BlockSpec((MM_TILE, d), lambda i: (i + first_tile, 0)),
                  pl.BlockSpec((1, d), lambda i: (0, 0)),
                  pl.BlockSpec((None, N_MOD, d), lambda i: (lay.mod_row_of_mm_tile(i + first_tile), 0, 0)),
                  pl.BlockSpec((d, LANES), lambda i: (0, 0)),
                  pl.BlockSpec((1, LANES), lambda i: (0, 0))],
        out_specs=[pl.BlockSpec((MM_TILE, d), lambda i: (i, 0)),
                   pl.BlockSpec((MM_TILE, LANES), lambda i: (i, 0)),
                   pl.BlockSpec((SUBLANES, LANES), lambda i: (0, 0))],
        out_shape=[jax.ShapeDtypeStruct((n_rows, d), F32),
                   jax.ShapeDtypeStruct((n_rows, LANES), F32),
                   jax.ShapeDtypeStruct((SUBLANES, LANES), F32)],
        scratch_shapes=[pltpu.VMEM((SUBLANES, LANES), F32)],
        compiler_params=_cparams(1),
        name="moe_router",
    )(x_all, gain.reshape(1, d), mods, w, b)


def _row_copy(src_hbm, src_row, buf, dst_row, sem):
    return pltpu.make_async_copy(src_hbm.at[pl.ds(src_row, 1), :], buf.at[pl.ds(dst_row, 1), :], sem)


def _expert_kernel(te_ref, src_ref, nt_ref, fl_hbm, wup_ref, wdn_ref, o_ref,
                   xbuf, sems, wup_bf, wdn_bf):
    i = pl.program_id(0)
    nt = nt_ref[0]
    slot = i % 2

    def gather(tile, slot_, start):
        def body(r, c):
            cp = _row_copy(fl_hbm, src_ref[tile * EXP_TILE + r], xbuf.at[slot_], r, sems.at[slot_])
            cp.start() if start else cp.wait()
            return c
        lax.fori_loop(0, EXP_TILE, body, 0)

    @pl.when(i == 0)
    def _():
        gather(0, 0, True)

    @pl.when(i + 1 < nt)
    def _():
        gather(i + 1, 1 - slot, True)

    @pl.when(i < nt)
    def _():
        expert_changed = (i == 0) | (te_ref[i] != te_ref[jnp.maximum(i - 1, 0)])

        @pl.when(expert_changed)
        def _():
            wup_bf[...] = wup_ref[0].astype(BF16)
            wdn_bf[...] = wdn_ref[0].astype(BF16)

        gather(i, slot, False)
        x = xbuf[slot].astype(BF16)
        hu = jnp.dot(x, wup_bf[...], preferred_element_type=F32)
        h = jax.nn.silu(hu[:, :D_EXPERT]) * hu[:, D_EXPERT:]
        o_ref[...] = jnp.dot(h.astype(BF16), wdn_bf[...], preferred_element_type=F32)

    @pl.when(i >= nt)
    def _():
        o_ref[...] = jnp.zeros_like(o_ref)


def _experts(fl, tile_expert, src_rows, n_tiles_used, w_up, w_down, n_tiles):
    d = fl.shape[1]
    last = lambda i, nt: jnp.minimum(i, nt[0] - 1)
    grid_spec = pltpu.PrefetchScalarGridSpec(
        num_scalar_prefetch=3,
        grid=(n_tiles,),
        in_specs=[pl.BlockSpec(memory_space=pl.ANY),
                  pl.BlockSpec((1, d, 2 * D_EXPERT), lambda i, te, src, nt: (te[last(i, nt)], 0, 0)),
                  pl.BlockSpec((1, D_EXPERT, d), lambda i, te, src, nt: (te[last(i, nt)], 0, 0))],
        out_specs=pl.BlockSpec((EXP_TILE, d), lambda i, te, src, nt: (i, 0)),
        scratch_shapes=[pltpu.VMEM((2, EXP_TILE, d), F32),
                        pltpu.SemaphoreType.DMA((2,)),
                        pltpu.VMEM((d, 2 * D_EXPERT), BF16),
                        pltpu.VMEM((D_EXPERT, d), BF16)])
    return pl.pallas_call(
        _expert_kernel,
        grid_spec=grid_spec,
        out_shape=jax.ShapeDtypeStruct((n_tiles * EXP_TILE, d), F32),
        compiler_params=_cparams(1),
        name="moe_experts",
    )(tile_expert, src_rows, n_tiles_used, fl, w_up, w_down)


def _combine_kernel(final, pos_ref, ys_hbm, x_ref, rec_ref, m_ref, gf_ref, o_ref, ybuf, sems):
    i = pl.program_id(0)
    n = pl.num_programs(0)
    slot = i % 2

    def gather(tile, slot_, start):
        def body(r, c):
            for k in range(2):
                cp = _row_copy(ys_hbm, pos_ref[(tile * ROW_TILE + r) * 2 + k], ybuf.at[slot_, k], r,
                               sems.at[slot_])
                cp.start() if start else cp.wait()
            return c
        lax.fori_loop(0, ROW_TILE, body, 0)

    @pl.when(i == 0)
    def _():
        gather(0, 0, True)

    @pl.when(i + 1 < n)
    def _():
        gather(i + 1, 1 - slot, True)

    gather(i, slot, False)
    rec = rec_ref[...]
    y = rec[:, R_W1:R_W1 + 1] * ybuf[slot, 0] + rec[:, R_W2:R_W2 + 1] * ybuf[slot, 1]
    x = x_ref[...] + m_ref[5:6, :] * y
    if final:
        r = lax.rsqrt(jnp.mean(x * x, axis=-1, keepdims=True) + EPS)
        x = (x * r) * gf_ref[...]
    o_ref[...] = x


def _combine(lay, pos, ys, x_all, rec, mods, gain_final, first_tile, final):
    d = x_all.shape[1]
    n_rows = rec.shape[0]
    off = first_tile * (MM_TILE // ROW_TILE)
    grid_spec = pltpu.PrefetchScalarGridSpec(
        num_scalar_prefetch=1,
        grid=(n_rows // ROW_TILE,),
        in_specs=[pl.BlockSpec(memory_space=pl.ANY),
                  pl.BlockSpec((ROW_TILE, d), lambda i, pos: (i + off, 0)),
                  pl.BlockSpec((ROW_TILE, LANES), lambda i, pos: (i, 0)),
                  pl.BlockSpec((None, N_MOD, d), lambda i, pos: (lay.mod_row_of_seq_tile(i + off), 0, 0)),
                  pl.BlockSpec((1, d), lambda i, pos: (0, 0))],
        out_specs=pl.BlockSpec((ROW_TILE, d), lambda i, pos: (i, 0)),
        scratch_shapes=[pltpu.VMEM((2, 2, ROW_TILE, d), F32),
                        pltpu.SemaphoreType.DMA((2,))])
    return pl.pallas_call(
        functools.partial(_combine_kernel, final),
        grid_spec=grid_spec,
        out_shape=jax.ShapeDtypeStruct((n_rows, d), F32),
        compiler_params=_cparams(1),
        name="moe_combine",
    )(pos, ys, x_all, rec, mods, gain_final.reshape(1, d))


def _dispatch_plan(rec, counts, n_tiles):
    n_rows = rec.shape[0]
    cnt = counts[0, :N_EXPERTS].astype(jnp.int32)
    tiles_per = (cnt + EXP_TILE - 1) // EXP_TILE
    tile_end = jnp.cumsum(tiles_per)
    offset = (tile_end - tiles_per) * EXP_TILE
    e_sel = rec[:, R_E1:R_E2 + 1].astype(jnp.int32)
    rank = rec[:, R_RANK1:R_RANK2 + 1].astype(jnp.int32)
    pos = (offset[e_sel] + rank).reshape(-1)
    token = jnp.repeat(jnp.arange(n_rows, dtype=jnp.int32), 2)
    src_rows = jnp.zeros((n_tiles * EXP_TILE,), jnp.int32).at[pos].set(token)
    n_used = tile_end[-1]
    tile_ids = jnp.minimum(jnp.arange(n_tiles, dtype=jnp.int32), n_used - 1)
    tile_expert = jnp.sum(tile_end[None, :] <= tile_ids[:, None], axis=1).astype(jnp.int32)
    return pos, src_rows, tile_expert, n_used.reshape(1).astype(jnp.int32)


def _moe(lay, x_all, gain, mods, w_grp, b_grp, w_exp, b_exp, w_up, w_down, gain_final, first_tile, final):
    fl, rec, counts = _router(lay, x_all, gain, mods, w_exp, b_exp, w_grp, b_grp, first_tile)
    n_rows = fl.shape[0]
    n_tiles = (2 * n_rows) // EXP_TILE + N_EXPERTS
    pos, src_rows, tile_expert, n_used = _dispatch_plan(rec, counts, n_tiles)
    ys = _experts(fl, tile_expert, src_rows, n_used, w_up, w_down, n_tiles)
    return _combine(lay, pos, ys, x_all, rec, mods, gain_final, first_tile, final)


def kernel(x, c, ctx, c_ctx, mod_w, mod_b, norm_mix, norm_ffn, w_in, w_out, rg_conv_w, rg_conv_b,
           rg_gate_w, rg_gate_b, rg_lam, sc_conv_w, pool_w, pool_b, pool_scale, da_lam, da_norm,
           moe_grp_w, moe_grp_b, moe_exp_w, moe_exp_b, moe_up, moe_down, norm_final):
    n_batch, seq, d = x.shape
    n_ctx = ctx.shape[1]
    depth = mod_w.shape[0]
    lay = _Layout(n_batch, n_ctx, seq)
    assert 1 + n_batch <= SUBLANES

    cc = jnp.zeros((SUBLANES, d), F32).at[0].set(c_ctx).at[1:1 + n_batch].set(c)
    mods_all = _modulation(cc, mod_w, mod_b)
    cos, sin = _rope_tables(lay)
    x_all = jnp.concatenate([ctx.reshape(n_batch * n_ctx, d), x.reshape(n_batch * seq, d)], axis=0)

    for l in range(depth):
        last = l == depth - 1
        lam_init = 0.8 - 0.6 * math.exp(-0.3 * l)
        mods = mods_all[l]

        proj = _in_projection(lay, x_all, norm_mix[l], mods, w_in[l].astype(BF16))
        hf = _rglru(lay, proj, rg_conv_w[l], rg_conv_b[l], rg_gate_w[l, 0], rg_gate_b[l, 0],
                    rg_lam[l, 0], reverse=False)
        ya = _rglru(lay, proj, rg_conv_w[l], rg_conv_b[l], rg_gate_w[l, 1], rg_gate_b[l, 1],
                    rg_lam[l, 1], reverse=True, hf=hf)
        yb, yc = _conv_pool(lay, proj, sc_conv_w[l], pool_w[l], pool_b[l], pool_scale[l])
        qs, kr, vb = _rope(lay, proj, cos, sin)
        yd = _attention(lay, qs, kr, vb, da_lam[l], da_norm[l], lam_init)
        x_all = _out_projection(lay, (ya, yb, yc, yd), w_out[l].astype(BF16), x_all, mods)

        first_tile = 1 if last else 0
        out = _moe(lay, x_all, norm_ffn[l], mods, moe_grp_w[l], moe_grp_b[l], moe_exp_w[l], moe_exp_b[l],
                   moe_up[l], moe_down[l], norm_final, first_tile, last)
        if last:
            return out.reshape(n_batch, seq, d)
        x_all = out
```

```python
import functools
import math

import jax
import jax.numpy as jnp
from jax import lax
from jax.experimental import pallas as pl
from jax.experimental.pallas import tpu as pltpu

F32 = jnp.float32
BF16 = jnp.bfloat16

W_GROUP = 512
N_PROJ = 9
RG_BLOCKS = 4
RG_BW = 128
RG_C = 8.0
POOL_WINDOWS = (2, 4, 8, 16)
POOL_GW = 128
DA_HEADS = 4
DA_HD = 64
DA_VD = 128
ROPE_THETA = 10000.0
GRID_W = 64
N_GROUPS = 4
EXP_PER_GROUP = 8
N_EXPERTS = 32
D_EXPERT = 512
N_MOD = 6
EPS = 1e-6

SUBLANES = 8
LANES = 128
ROW_TILE = 256
MM_TILE = 512
MAX_KV_CHUNK = 2816
ONES_ROWS = 16
EXP_TILE = 256
DMA_UNROLL = 8
VMEM_LIMIT = 56 * 1024 * 1024


def _cparams(n_axes):
    return pltpu.CompilerParams(dimension_semantics=("arbitrary",) * n_axes,
                                vmem_limit_bytes=VMEM_LIMIT)


def _mod_kernel(c_ref, w_ref, b_ref, o_ref):
    c = c_ref[...]
    s = c * jax.nn.sigmoid(c)
    o_ref[0] = jnp.dot(s.astype(BF16), w_ref[0].astype(BF16), preferred_element_type=F32) + b_ref[0]


def _modulation(cc, mod_w, mod_b):
    depth, d, n = mod_w.shape
    tn = 1024
    out = pl.pallas_call(
        _mod_kernel,
        grid=(depth, n // tn),
        in_specs=[pl.BlockSpec((SUBLANES, d), lambda l, j: (0, 0)),
                  pl.BlockSpec((1, d, tn), lambda l, j: (l, 0, j)),
                  pl.BlockSpec((1, 1, tn), lambda l, j: (l, 0, j))],
        out_specs=pl.BlockSpec((1, SUBLANES, tn), lambda l, j: (l, 0, j)),
        out_shape=jax.ShapeDtypeStruct((depth, SUBLANES, n), F32),
        compiler_params=_cparams(2),
        name="modulation",
    )(cc, mod_w, mod_b.reshape(depth, 1, n))
    return out.reshape(depth, SUBLANES, N_MOD, d)


def _rms_mod(x, g, shift, scale):
    r = lax.rsqrt(jnp.mean(x * x, axis=-1, keepdims=True) + EPS)
    return (x * r) * g * (1.0 + scale) + shift


class _Layout:
    def __init__(self, n_batch, n_ctx, seq):
        assert n_ctx == ROW_TILE, "one sequence tile per context sequence"
        assert n_batch * n_ctx == MM_TILE, "context rows fill exactly one projection tile"
        assert seq % MM_TILE == 0 and seq % GRID_W == 0
        self.b = n_batch
        self.n_ctx = n_ctx
        self.seq = seq
        self.rows = n_batch * (n_ctx + seq)
        self.nl = seq // ROW_TILE
        self.n_tiles = self.rows // ROW_TILE

    def mod_row_of_seq_tile(self, i):
        return jnp.where(i < self.b, 0, 1 + (i - self.b) // self.nl)

    def mod_row_of_mm_tile(self, i):
        return jnp.where(i < 1, 0, 1 + (i - 1) // (self.seq // MM_TILE))


def _inproj_kernel(x_ref, g_ref, m_ref, w_ref, o_ref, h_scr):
    @pl.when(pl.program_id(1) == 0)
    def _():
        h = _rms_mod(x_ref[...], g_ref[...], m_ref[0:1, :], m_ref[1:2, :])
        h_scr[...] = h.astype(BF16)

    o_ref[...] = jnp.dot(h_scr[...], w_ref[...], preferred_element_type=F32)


def _in_projection(lay, x_all, gain, mods, w_bf):
    t, d = x_all.shape
    n = w_bf.shape[1]
    tn = 1536
    return pl.pallas_call(
        _inproj_kernel,
        grid=(t // MM_TILE, n // tn),
        in_specs=[pl.BlockSpec((MM_TILE, d), lambda i, j: (i, 0)),
                  pl.BlockSpec((1, d), lambda i, j: (0, 0)),
                  pl.BlockSpec((None, N_MOD, d), lambda i, j: (lay.mod_row_of_mm_tile(i), 0, 0)),
                  pl.BlockSpec((d, tn), lambda i, j: (0, j))],
        out_specs=pl.BlockSpec((MM_TILE, tn), lambda i, j: (i, j)),
        out_shape=jax.ShapeDtypeStruct((t, n), F32),
        scratch_shapes=[pltpu.VMEM((MM_TILE, d), BF16)],
        compiler_params=_cparams(2),
        name="in_projection",
    )(x_all, gain.reshape(1, d), mods, w_bf)


def _fill_ext(ext, prev_ref, x, next_ref, is_first, is_last):
    ext[0:SUBLANES, :] = jnp.where(is_first, 0.0, prev_ref[...])
    ext[SUBLANES:SUBLANES + ROW_TILE, :] = x
    ext[SUBLANES + ROW_TILE:, :] = jnp.where(is_last, 0.0, next_ref[...])


def _shifted(ext, off, cols=slice(None)):
    return ext[SUBLANES + off:SUBLANES + off + ROW_TILE, cols]


def _rglru_kernel(reverse, n_lat_tiles, *refs):
    if reverse:
        (x_ref, xp_ref, xn_ref, cw_ref, cb_ref, gw_ref, gb_ref, lam_ref, hf_ref, ag_ref,
         o_ref, ext, a_scr, b_scr, carry) = refs
    else:
        (x_ref, xp_ref, xn_ref, cw_ref, cb_ref, gw_ref, gb_ref, lam_ref,
         o_ref, ext, a_scr, b_scr, carry) = refs
    s = pl.program_id(1)
    if reverse:
        is_first = (s == 0) | (s == n_lat_tiles)
        is_last = (s == 0) | (s == 1)
    else:
        is_first = (s == 0) | (s == 1)
        is_last = (s == 0) | (s == n_lat_tiles)

    @pl.when(s == 0)
    def _():
        carry[...] = jnp.zeros_like(carry)

    _fill_ext(ext, xp_ref, x_ref[...], xn_ref, is_first, is_last)
    u = cb_ref[...] + sum(cw_ref[k:k + 1, :] * _shifted(ext, k - 2) for k in range(4))

    sp = -lam_ref[...]
    sp = jnp.maximum(sp, 0.0) + jnp.log1p(jnp.exp(-jnp.abs(sp)))
    for n in range(RG_BLOCKS):
        cols = slice(n * RG_BW, (n + 1) * RG_BW)
        un = u[:, cols]
        g = jnp.dot(un.astype(BF16), gw_ref[n], preferred_element_type=F32)
        r = jax.nn.sigmoid(g[:, :RG_BW] + gb_ref[0:1, cols])
        gi = jax.nn.sigmoid(g[:, RG_BW:] + gb_ref[1:2, cols])
        log_a = -RG_C * r * sp[:, cols]
        a_scr[:, cols] = jnp.exp(log_a)
        th = jnp.tanh(log_a)
        b_scr[:, cols] = jnp.sqrt(-2.0 * th / (1.0 - th)) * (gi * un)

    row = lax.broadcasted_iota(jnp.int32, (SUBLANES, W_GROUP), 0)
    n_groups = ROW_TILE // SUBLANES

    def group(gidx, h_prev):
        g0 = (n_groups - 1 - gidx) if reverse else gidx
        start = pl.multiple_of(g0 * SUBLANES, SUBLANES)
        a8 = a_scr[pl.ds(start, SUBLANES), :]
        b8 = b_scr[pl.ds(start, SUBLANES), :]
        for sh in (1, 2, 4):
            rot = (SUBLANES - sh) if reverse else sh
            a_sh = pltpu.roll(a8, rot, 0)
            b_sh = pltpu.roll(b8, rot, 0)
            valid = (row < SUBLANES - sh) if reverse else (row >= sh)
            b8 = jnp.where(valid, a8 * b_sh + b8, b8)
            a8 = jnp.where(valid, a8 * a_sh, a8)
        h8 = a8 * h_prev + b8
        a_scr[pl.ds(start, SUBLANES), :] = h8
        edge = 0 if reverse else SUBLANES - 1
        return h8[edge:edge + 1, :]

    carry[...] = lax.fori_loop(0, n_groups, group, carry[...])

    h = a_scr[...]
    if reverse:
        o_ref[...] = (jax.nn.gelu(ag_ref[...]) * (hf_ref[...] + h)).astype(o_ref.dtype)
    else:
        o_ref[...] = h


def _rglru(lay, proj, conv_w, conv_b, gate_w, gate_b, lam, reverse, hf=None):
    t = proj.shape[0]
    nl = lay.nl
    nb8 = ROW_TILE // SUBLANES

    def tile(b, s):
        lat = (nl - s) if reverse else (s - 1)
        return jnp.where(s == 0, b, lay.b + b * nl + lat)

    main = lambda c: pl.BlockSpec((ROW_TILE, W_GROUP), lambda b, s: (tile(b, s), c))
    prev = pl.BlockSpec((SUBLANES, W_GROUP), lambda b, s: (jnp.maximum(tile(b, s) * nb8 - 1, 0), 0))
    nxt = pl.BlockSpec((SUBLANES, W_GROUP),
                       lambda b, s: (jnp.minimum((tile(b, s) + 1) * nb8, t // SUBLANES - 1), 0))
    const = lambda shape: pl.BlockSpec(shape, lambda b, s: (0,) * len(shape))
    gw = jnp.transpose(gate_w, (1, 2, 0, 3)).reshape(RG_BLOCKS, RG_BW, 2 * RG_BW).astype(BF16)
    gb = gate_b.reshape(2, W_GROUP)
    in_specs = [main(0), prev, nxt, const((4, W_GROUP)), const((1, W_GROUP)),
                const((RG_BLOCKS, RG_BW, 2 * RG_BW)), const((2, W_GROUP)), const((1, W_GROUP))]
    args = [proj, proj, proj, conv_w, conv_b.reshape(1, W_GROUP), gw, gb, lam.reshape(1, W_GROUP)]
    if reverse:
        in_specs += [pl.BlockSpec((ROW_TILE, W_GROUP), lambda b, s: (tile(b, s), 0)), main(1)]
        args += [hf, proj]
    return pl.pallas_call(
        functools.partial(_rglru_kernel, reverse, nl),
        grid=(lay.b, 1 + nl),
        in_specs=in_specs,
        out_specs=pl.BlockSpec((ROW_TILE, W_GROUP), lambda b, s: (tile(b, s), 0)),
        out_shape=jax.ShapeDtypeStruct((t, W_GROUP), BF16 if reverse else F32),
        scratch_shapes=[pltpu.VMEM((ROW_TILE + 2 * SUBLANES, W_GROUP), F32),
                        pltpu.VMEM((ROW_TILE, W_GROUP), F32),
                        pltpu.VMEM((ROW_TILE, W_GROUP), F32),
                        pltpu.VMEM((1, W_GROUP), F32)],
        compiler_params=_cparams(2),
        name="rglru_bwd" if reverse else "rglru_fwd",
    )(*args)


def _conv_pool_kernel(lay, bb_ref, bc_ref, bcp_ref, bcn_ref, bx_ref, bxp_ref, bxn_ref,
                      cx_ref, cxp_ref, cxn_ref, scw_ref, pw_ref, pb_ref, ps_ref,
                      yb_ref, yc_ref, ext):
    i = pl.program_id(0)
    j = jnp.maximum(i - lay.b, 0) % lay.nl
    is_ctx = i < lay.b
    is_first = is_ctx | (j == 0)
    is_last = is_ctx | (j == lay.nl - 1)

    ext[0:SUBLANES, :] = jnp.where(is_first, 0.0, bcp_ref[...] * bxp_ref[...])
    ext[SUBLANES:SUBLANES + ROW_TILE, :] = bc_ref[...] * bx_ref[...]
    ext[SUBLANES + ROW_TILE:, :] = jnp.where(is_last, 0.0, bcn_ref[...] * bxn_ref[...])
    conv = sum(scw_ref[k:k + 1, :] * _shifted(ext, k - 1) for k in range(3))
    yb_ref[...] = (bb_ref[...] * conv).astype(yb_ref.dtype)

    x = cx_ref[...]
    _fill_ext(ext, cxp_ref, x, cxn_ref, is_first, is_last)
    seq_len = jnp.where(is_ctx, lay.n_ctx, lay.seq)
    pos = jnp.where(is_ctx, 0, j * ROW_TILE) + lax.broadcasted_iota(jnp.int32, (ROW_TILE, 1), 0)
    for gi, win in enumerate(POOL_WINDOWS):
        cols = slice(gi * POOL_GW, (gi + 1) * POOL_GW)
        half = win // 2
        tot = sum(_shifted(ext, off, cols) for off in range(-half, half))
        cnt = jnp.minimum(pos + half, seq_len) - jnp.maximum(pos - half, 0)
        dlt = tot / cnt.astype(F32) - x[:, cols]
        y = jnp.dot(dlt.astype(BF16), pw_ref[gi].astype(BF16), preferred_element_type=F32)
        yc_ref[:, cols] = ((y + pb_ref[gi:gi + 1, :]) * ps_ref[:, cols]).astype(yc_ref.dtype)


def _conv_pool(lay, proj, sc_w, pool_w, pool_b, pool_scale):
    t = proj.shape[0]
    nb8 = ROW_TILE // SUBLANES
    main = lambda c: pl.BlockSpec((ROW_TILE, W_GROUP), lambda i: (i, c))
    prev = lambda c: pl.BlockSpec((SUBLANES, W_GROUP), lambda i: (jnp.maximum(i * nb8 - 1, 0), c))
    nxt = lambda c: pl.BlockSpec((SUBLANES, W_GROUP),
                                 lambda i: (jnp.minimum((i + 1) * nb8, t // SUBLANES - 1), c))
    const = lambda shape: pl.BlockSpec(shape, lambda i: (0,) * len(shape))
    n_pool = len(POOL_WINDOWS)
    out = jax.ShapeDtypeStruct((t, W_GROUP), BF16)
    return pl.pallas_call(
        functools.partial(_conv_pool_kernel, lay),
        grid=(t // ROW_TILE,),
        in_specs=[main(2), main(3), prev(3), nxt(3), main(4), prev(4), nxt(4),
                  main(5), prev(5), nxt(5), const((3, W_GROUP)),
                  const((n_pool, POOL_GW, POOL_GW)), const((n_pool, POOL_GW)), const((1, W_GROUP))],
        out_specs=[pl.BlockSpec((ROW_TILE, W_GROUP), lambda i: (i, 0))] * 2,
        out_shape=[out, out],
        scratch_shapes=[pltpu.VMEM((ROW_TILE + 2 * SUBLANES, W_GROUP), F32)],
        compiler_params=_cparams(1),
        name="conv_pool",
    )(*([proj] * 10), sc_w, pool_w, pool_b, pool_scale.reshape(1, W_GROUP))


def _rope_tables(lay):
    nf = DA_HD // 4
    inv = ROPE_THETA ** (-jnp.arange(nf, dtype=F32) / nf)
    pos = jnp.arange(lay.seq, dtype=jnp.int32)
    ang_r = (pos // GRID_W).astype(F32)[:, None] * inv
    ang_c = (pos % GRID_W).astype(F32)[:, None] * inv
    cos = jnp.concatenate([jnp.cos(ang_r)] * 2 + [jnp.cos(ang_c)] * 2, axis=-1)
    sin = jnp.concatenate([-jnp.sin(ang_r), jnp.sin(ang_r), -jnp.sin(ang_c), jnp.sin(ang_c)], axis=-1)
    cos = jnp.concatenate([jnp.ones((ROW_TILE, DA_HD), F32), cos], axis=0)
    sin = jnp.concatenate([jnp.zeros((ROW_TILE, DA_HD), F32), sin], axis=0)
    return jnp.tile(cos, (1, 2)), jnp.tile(sin, (1, 2))


def _rope_kernel(q_ref, k_ref, v_ref, cos_ref, sin_ref, qo_ref, ko_ref, vto_ref):
    cos = cos_ref[...]
    sin = sin_ref[...]
    nf = DA_HD // 4
    lane = lax.broadcasted_iota(jnp.int32, (ROW_TILE, LANES), 1)
    first_half = (lane & (2 * nf - 1)) < nf

    def rot(x):
        partner = jnp.where(first_half, pltpu.roll(x, LANES - nf, 1), pltpu.roll(x, nf, 1))
        return x * cos + partner * sin

    for h in range(DA_HEADS):
        cols = slice(h * DA_VD, (h + 1) * DA_VD)
        qo_ref[:, cols] = (rot(q_ref[:, cols]) * (DA_HD ** -0.5)).astype(BF16)
        ko_ref[:, cols] = rot(k_ref[:, cols]).astype(BF16)
        vto_ref[cols, :] = v_ref[:, cols].T.astype(BF16)


def _rope(lay, proj, cos, sin):
    t = proj.shape[0]
    nl = lay.nl
    main = lambda c: pl.BlockSpec((ROW_TILE, W_GROUP), lambda i: (i, c))
    tab = pl.BlockSpec((ROW_TILE, LANES), lambda i: (jnp.where(i < lay.b, 0, 1 + (i - lay.b) % nl), 0))

    def key_tile(i):
        j = jnp.maximum(i - lay.b, 0)
        return jnp.where(i < lay.b, i * (nl + 1), (j // nl) * (nl + 1) + 1 + j % nl)

    return pl.pallas_call(
        _rope_kernel,
        grid=(t // ROW_TILE,),
        in_specs=[main(6), main(7), main(8), tab, tab],
        out_specs=[pl.BlockSpec((ROW_TILE, W_GROUP), lambda i: (i, 0)),
                   pl.BlockSpec((ROW_TILE, W_GROUP), lambda i: (key_tile(i), 0)),
                   pl.BlockSpec((W_GROUP, ROW_TILE), lambda i: (0, key_tile(i)))],
        out_shape=[jax.ShapeDtypeStruct((t, W_GROUP), BF16),
                   jax.ShapeDtypeStruct((t, W_GROUP), BF16),
                   jax.ShapeDtypeStruct((W_GROUP, t), BF16)],
        compiler_params=_cparams(1),
        name="rope",
    )(proj, proj, proj, cos, sin)


def _kv_chunk(n_keys):
    return max(c for c in range(ROW_TILE, MAX_KV_CHUNK + 1, ROW_TILE) if n_keys % c == 0)


def _attn_kernel(lam_init, n_ctx, chunk, q_ref, k_ref, vt_ref, lp_ref, g_ref, o_ref, qqt, m_scr, acc):
    qi = pl.program_id(2)
    tq = q_ref.shape[0]
    n_keys = k_ref.shape[0]

    qt = q_ref[...].astype(F32).T
    feat = lax.broadcasted_iota(jnp.int32, qt.shape, 0)
    qqt[:, 0:tq] = jnp.where(feat < DA_HD, qt, 0.0).astype(BF16)
    qqt[:, tq:] = jnp.where(feat >= DA_HD, qt, 0.0).astype(BF16)
    m_scr[...] = jnp.full_like(m_scr, -jnp.inf)
    acc[...] = jnp.zeros_like(acc)

    def scores(start, size):
        s = jnp.dot(k_ref[start:start + size, :], qqt[...], preferred_element_type=F32)
        return s, jnp.max(s, axis=0, keepdims=True)

    def accumulate(start, size, s, s_max):
        m_prev = m_scr[0:1, :]
        m_new = jnp.maximum(m_prev, s_max)
        alpha = jnp.exp(m_prev - m_new)
        p = jnp.exp(s - m_new).astype(BF16)
        vt = jnp.concatenate([vt_ref[:, start:start + size], jnp.ones((ONES_ROWS, size), BF16)], axis=0)
        acc[...] = alpha * acc[...] + jnp.dot(vt, p, preferred_element_type=F32)
        m_scr[...] = jnp.broadcast_to(m_new, m_scr.shape)

    @pl.when(qi == 0)
    def _():
        accumulate(0, n_ctx, *scores(0, n_ctx))

    @pl.when(qi > 0)
    def _():
        n_chunks = n_keys // chunk
        cur = scores(0, chunk)
        for c in range(n_chunks):
            nxt = scores((c + 1) * chunk, chunk) if c + 1 < n_chunks else None
            accumulate(c * chunk, chunk, *cur)
            cur = nxt

    lp = lp_ref[...]
    lam = (jnp.exp(jnp.sum(lp[0:1] * lp[1:2], axis=-1, keepdims=True))
           - jnp.exp(jnp.sum(lp[2:3] * lp[3:4], axis=-1, keepdims=True)) + lam_init)
    den = acc[DA_VD:DA_VD + 1, :]
    ot = acc[0:DA_VD, 0:tq] / den[:, 0:tq] - lam * (acc[0:DA_VD, tq:] / den[:, tq:])
    o = ot.T
    r = lax.rsqrt(jnp.mean(o * o, axis=-1, keepdims=True) + EPS)
    o_ref[...] = ((o * r) * g_ref[...] * (1.0 - lam_init)).astype(o_ref.dtype)


def _attention(lay, qs, kr, vt, da_lam, da_norm, lam_init):
    t = qs.shape[0]
    nl = lay.nl
    tq = ROW_TILE
    n_keys = lay.n_ctx + lay.seq
    q_spec = pl.BlockSpec((tq, DA_VD), lambda b, h, qi: (jnp.where(qi == 0, b, lay.b + b * nl + qi - 1), h))
    const = lambda shape: pl.BlockSpec(shape, lambda b, h, qi: (0,) * len(shape))
    return pl.pallas_call(
        functools.partial(_attn_kernel, lam_init, lay.n_ctx, _kv_chunk(n_keys)),
        grid=(lay.b, DA_HEADS, 1 + nl),
        in_specs=[q_spec,
                  pl.BlockSpec((n_keys, DA_VD), lambda b, h, qi: (b, h)),
                  pl.BlockSpec((DA_VD, n_keys), lambda b, h, qi: (h, b)),
                  const((4, DA_HD)), const((1, DA_VD))],
        out_specs=q_spec,
        out_shape=jax.ShapeDtypeStruct((t, W_GROUP), BF16),
        scratch_shapes=[pltpu.VMEM((DA_VD, 2 * tq), BF16),
                        pltpu.VMEM((SUBLANES, 2 * tq), F32),
                        pltpu.VMEM((DA_VD + ONES_ROWS, 2 * tq), F32)],
        compiler_params=_cparams(3),
        name="attention",
    )(qs, kr, vt, da_lam, da_norm.reshape(1, DA_VD))


def _outproj_kernel(ya_ref, yb_ref, yc_ref, yd_ref, w_ref, x_ref, m_ref, o_ref):
    y = jnp.concatenate([ya_ref[...], yb_ref[...], yc_ref[...], yd_ref[...]], axis=-1)
    o_ref[...] = x_ref[...] + m_ref[2:3, :] * jnp.dot(y, w_ref[...], preferred_element_type=F32)


def _out_projection(lay, ys, w_bf, x_all, mods):
    t, d = x_all.shape
    tn = 1024
    y_spec = pl.BlockSpec((MM_TILE, W_GROUP), lambda j, i: (i, 0))
    return pl.pallas_call(
        _outproj_kernel,
        grid=(d // tn, t // MM_TILE),
        in_specs=[y_spec] * 4 + [
            pl.BlockSpec((d, tn), lambda j, i: (0, j)),
            pl.BlockSpec((MM_TILE, tn), lambda j, i: (i, j)),
            pl.BlockSpec((None, N_MOD, tn), lambda j, i: (lay.mod_row_of_mm_tile(i), 0, j))],
        out_specs=pl.BlockSpec((MM_TILE, tn), lambda j, i: (i, j)),
        out_shape=jax.ShapeDtypeStruct((t, d), F32),
        compiler_params=_cparams(2),
        name="out_projection",
    )(*ys, w_bf, x_all, mods)


R_E1, R_E2, R_W1, R_W2, R_RANK1, R_RANK2 = range(6)


def _router_kernel(x_ref, g_ref, m_ref, w_ref, b_ref, fl_ref, rec_ref, cnt_ref, counts):
    i = pl.program_id(0)

    @pl.when(i == 0)
    def _():
        counts[...] = jnp.zeros_like(counts)

    fl = _rms_mod(x_ref[...], g_ref[...], m_ref[3:4, :], m_ref[4:5, :])
    fl_ref[...] = fl
    logit = jnp.dot(fl, w_ref[...], preferred_element_type=F32, precision=lax.Precision.HIGHEST) + b_ref[...]
    tm = logit.shape[0]
    lane = lax.broadcasted_iota(jnp.int32, (tm, LANES), 1).astype(F32)
    neg = -jnp.inf

    def first_argmax(vals):
        top = jnp.max(vals, axis=-1, keepdims=True)
        return top, jnp.min(jnp.where(vals == top, lane, float(LANES)), axis=-1, keepdims=True)

    is_grp = (lane >= N_EXPERTS) & (lane < N_EXPERTS + N_GROUPS)
    g_top, g_lane = first_argmax(jnp.where(is_grp, logit, neg))
    g_w = 1.0 / jnp.sum(jnp.where(is_grp, jnp.exp(logit - g_top), 0.0), axis=-1, keepdims=True)

    grp_start = (g_lane - N_EXPERTS) * EXP_PER_GROUP
    in_grp = (lane >= grp_start) & (lane < grp_start + EXP_PER_GROUP)
    v1, e1 = first_argmax(jnp.where(in_grp, logit, neg))
    v2, e2 = first_argmax(jnp.where(in_grp & (lane != e1), logit, neg))
    z = jnp.exp(v2 - v1)
    w1 = g_w / (1.0 + z)
    w2 = g_w * z / (1.0 + z)

    oh1 = lane == e1
    oh2 = lane == e2
    oh = jnp.where(oh1 | oh2, 1.0, 0.0).astype(BF16)
    r_i = lax.broadcasted_iota(jnp.int32, (tm, tm), 0)
    c_i = lax.broadcasted_iota(jnp.int32, (tm, tm), 1)
    below = jnp.where(c_i < r_i, 1.0, 0.0).astype(BF16)
    before = counts[0:1, :] + jnp.dot(below, oh, preferred_element_type=F32)
    rank1 = jnp.sum(jnp.where(oh1, before, 0.0), axis=-1, keepdims=True)
    rank2 = jnp.sum(jnp.where(oh2, before, 0.0), axis=-1, keepdims=True)
    counts[...] = counts[...] + jnp.sum(oh.astype(F32), axis=0, keepdims=True)
    cnt_ref[...] = counts[...]

    rec = jnp.zeros((tm, LANES), F32)
    for slot, val in ((R_E1, e1), (R_E2, e2), (R_W1, w1), (R_W2, w2), (R_RANK1, rank1), (R_RANK2, rank2)):
        rec = jnp.where(lane == slot, val, rec)
    rec_ref[...] = rec


def _router(lay, x_all, gain, mods, w_exp, b_exp, w_grp, b_grp, first_tile):
    t, d = x_all.shape
    n_rows = t - first_tile * MM_TILE
    pad = LANES - N_EXPERTS - N_GROUPS
    w = jnp.concatenate([w_exp, w_grp, jnp.zeros((d, pad), F32)], axis=1)
    b = jnp.concatenate([b_exp, b_grp, jnp.zeros((pad,), F32)]).reshape(1, LANES)
    return pl.pallas_call(
        _router_kernel,
        grid=(n_rows // MM_TILE,),
        in_specs=[pl.BlockSpec((MM_TILE, d), lambda i: (i + first_tile, 0)),
                  pl.BlockSpec((1, d), lambda i: (0, 0)),
                  pl.BlockSpec((None, N_MOD, d), lambda i: (lay.mod_row_of_mm_tile(i + first_tile), 0, 0)),
                  pl.BlockSpec((d, LANES), lambda i: (0, 0)),
                  pl.BlockSpec((1, LANES), lambda i: (0, 0))],
        out_specs=[pl.BlockSpec((MM_TILE, d), lambda i: (i, 0)),
                   pl.BlockSpec((MM_TILE, LANES), lambda i: (i, 0)),
                   pl.BlockSpec((SUBLANES, LANES), lambda i: (0, 0))],
        out_shape=[jax.ShapeDtypeStruct((n_rows, d), F32),
                   jax.ShapeDtypeStruct((n_rows, LANES), F32),
                   jax.ShapeDtypeStruct((SUBLANES, LANES), F32)],
        scratch_shapes=[pltpu.VMEM((SUBLANES, LANES), F32)],
        compiler_params=_cparams(1),
        name="moe_router",
    )(x_all, gain.reshape(1, d), mods, w, b)


def _row_copy(src_hbm, src_row, buf, dst_row, sem):
    return pltpu.make_async_copy(src_hbm.at[pl.ds(src_row, 1), :], buf.at[pl.ds(dst_row, 1), :], sem)


def _expert_kernel(te_ref, src_ref, nt_ref, fl_hbm, wup_ref, wdn_ref, o_ref,
                   xbuf, sems, wup_bf, wdn_bf):
    i = pl.program_id(0)
    nt = nt_ref[0]
    slot = i % 2

    def start_gather(tile, slot_):
        def body(r, c):
            _row_copy(fl_hbm, src_ref[tile * EXP_TILE + r], xbuf.at[slot_], r, sems.at[slot_]).start()
            return c
        lax.fori_loop(0, EXP_TILE, body, 0, unroll=DMA_UNROLL)

    def wait_gather(slot_):
        for r in range(EXP_TILE):
            _row_copy(fl_hbm, 0, xbuf.at[slot_], r, sems.at[slot_]).wait()

    @pl.when(i == 0)
    def _():
        start_gather(0, 0)

    @pl.when(i + 1 < nt)
    def _():
        start_gather(i + 1, 1 - slot)

    @pl.when(i < nt)
    def _():
        expert_changed = (i == 0) | (te_ref[i] != te_ref[jnp.maximum(i - 1, 0)])

        @pl.when(expert_changed)
        def _():
            wup_bf[...] = wup_ref[0].astype(BF16)
            wdn_bf[...] = wdn_ref[0].astype(BF16)

        wait_gather(slot)
        x = xbuf[slot].astype(BF16)
        hu = jnp.dot(x, wup_bf[...], preferred_element_type=F32)
        h = jax.nn.silu(hu[:, :D_EXPERT]) * hu[:, D_EXPERT:]
        o_ref[...] = jnp.dot(h.astype(BF16), wdn_bf[...], preferred_element_type=F32)

    @pl.when(i >= nt)
    def _():
        o_ref[...] = jnp.zeros_like(o_ref)


def _experts(fl, tile_expert, src_rows, n_tiles_used, w_up, w_down, n_tiles):
    d = fl.shape[1]
    last = lambda i, nt: jnp.minimum(i, nt[0] - 1)
    grid_spec = pltpu.PrefetchScalarGridSpec(
        num_scalar_prefetch=3,
        grid=(n_tiles,),
        in_specs=[pl.BlockSpec(memory_space=pl.ANY),
                  pl.BlockSpec((1, d, 2 * D_EXPERT), lambda i, te, src, nt: (te[last(i, nt)], 0, 0)),
                  pl.BlockSpec((1, D_EXPERT, d), lambda i, te, src, nt: (te[last(i, nt)], 0, 0))],
        out_specs=pl.BlockSpec((EXP_TILE, d), lambda i, te, src, nt: (i, 0)),
        scratch_shapes=[pltpu.VMEM((2, EXP_TILE, d), F32),
                        pltpu.SemaphoreType.DMA((2,)),
                        pltpu.VMEM((d, 2 * D_EXPERT), BF16),
                        pltpu.VMEM((D_EXPERT, d), BF16)])
    return pl.pallas_call(
        _expert_kernel,
        grid_spec=grid_spec,
        out_shape=jax.ShapeDtypeStruct((n_tiles * EXP_TILE, d), F32),
        compiler_params=_cparams(1),
        name="moe_experts",
    )(tile_expert, src_rows, n_tiles_used, fl, w_up, w_down)


def _combine_kernel(final, pos_ref, ys_hbm, x_ref, rec_ref, m_ref, gf_ref, o_ref, ybuf, sems):
    i = pl.program_id(0)
    n = pl.num_programs(0)
    slot = i % 2

    def start_gather(tile, slot_):
        def body(r, c):
            for k in range(2):
                _row_copy(ys_hbm, pos_ref[(tile * ROW_TILE + r) * 2 + k], ybuf.at[slot_, k], r,
                          sems.at[slot_]).start()
            return c
        lax.fori_loop(0, ROW_TILE, body, 0, unroll=DMA_UNROLL)

    def wait_gather(slot_):
        for r in range(2 * ROW_TILE):
            _row_copy(ys_hbm, 0, ybuf.at[slot_, r % 2], r // 2, sems.at[slot_]).wait()

    @pl.when(i == 0)
    def _():
        start_gather(0, 0)

    @pl.when(i + 1 < n)
    def _():
        start_gather(i + 1, 1 - slot)

    wait_gather(slot)
    rec = rec_ref[...]
    y = rec[:, R_W1:R_W1 + 1] * ybuf[slot, 0] + rec[:, R_W2:R_W2 + 1] * ybuf[slot, 1]
    x = x_ref[...] + m_ref[5:6, :] * y
    if final:
        r = lax.rsqrt(jnp.mean(x * x, axis=-1, keepdims=True) + EPS)
        x = (x * r) * gf_ref[...]
    o_ref[...] = x


def _combine(lay, pos, ys, x_all, rec, mods, gain_final, first_tile, final):
    d = x_all.shape[1]
    n_rows = rec.shape[0]
    off = first_tile * (MM_TILE // ROW_TILE)
    grid_spec = pltpu.PrefetchScalarGridSpec(
        num_scalar_prefetch=1,
        grid=(n_rows // ROW_TILE,),
        in_specs=[pl.BlockSpec(memory_space=pl.ANY),
                  pl.BlockSpec((ROW_TILE, d), lambda i, pos: (i + off, 0)),
                  pl.BlockSpec((ROW_TILE, LANES), lambda i, pos: (i, 0)),
                  pl.BlockSpec((None, N_MOD, d), lambda i, pos: (lay.mod_row_of_seq_tile(i + off), 0, 0)),
                  pl.BlockSpec((1, d), lambda i, pos: (0, 0))],
        out_specs=pl.BlockSpec((ROW_TILE, d), lambda i, pos: (i, 0)),
        scratch_shapes=[pltpu.VMEM((2, 2, ROW_TILE, d), F32),
                        pltpu.SemaphoreType.DMA((2,))])
    return pl.pallas_call(
        functools.partial(_combine_kernel, final),
        grid_spec=grid_spec,
        out_shape=jax.ShapeDtypeStruct((n_rows, d), F32),
        compiler_params=_cparams(1),
        name="moe_combine",
    )(pos, ys, x_all, rec, mods, gain_final.reshape(1, d))


def _dispatch_plan(rec, counts, n_tiles):
    n_rows = rec.shape[0]
    cnt = counts[0, :N_EXPERTS].astype(jnp.int32)
    tiles_per = (cnt + EXP_TILE - 1) // EXP_TILE
    tile_end = jnp.cumsum(tiles_per)
    offset = (tile_end - tiles_per) * EXP_TILE
    e_sel = rec[:, R_E1:R_E2 + 1].astype(jnp.int32)
    rank = rec[:, R_RANK1:R_RANK2 + 1].astype(jnp.int32)
    pos = (offset[e_sel] + rank).reshape(-1)
    token = jnp.repeat(jnp.arange(n_rows, dtype=jnp.int32), 2)
    src_rows = jnp.zeros((n_tiles * EXP_TILE,), jnp.int32).at[pos].set(token)
    n_used = tile_end[-1]
    tile_ids = jnp.minimum(jnp.arange(n_tiles, dtype=jnp.int32), n_used - 1)
    tile_expert = jnp.sum(tile_end[None, :] <= tile_ids[:, None], axis=1).astype(jnp.int32)
    return pos, src_rows, tile_expert, n_used.reshape(1).astype(jnp.int32)


def _moe(lay, x_all, gain, mods, w_grp, b_grp, w_exp, b_exp, w_up, w_down, gain_final, first_tile, final):
    fl, rec, counts = _router(lay, x_all, gain, mods, w_exp, b_exp, w_grp, b_grp, first_tile)
    n_rows = fl.shape[0]
    n_tiles = (2 * n_rows) // EXP_TILE + N_EXPERTS
    pos, src_rows, tile_expert, n_used = _dispatch_plan(rec, counts, n_tiles)
    ys = _experts(fl, tile_expert, src_rows, n_used, w_up, w_down, n_tiles)
    return _combine(lay, pos, ys, x_all, rec, mods, gain_final, first_tile, final)


def kernel(x, c, ctx, c_ctx, mod_w, mod_b, norm_mix, norm_ffn, w_in, w_out, rg_conv_w, rg_conv_b,
           rg_gate_w, rg_gate_b, rg_lam, sc_conv_w, pool_w, pool_b, pool_scale, da_lam, da_norm,
           moe_grp_w, moe_grp_b, moe_exp_w, moe_exp_b, moe_up, moe_down, norm_final):
    n_batch, seq, d = x.shape
    n_ctx = ctx.shape[1]
    depth = mod_w.shape[0]
    lay = _Layout(n_batch, n_ctx, seq)
    assert 1 + n_batch <= SUBLANES

    cc = jnp.zeros((SUBLANES, d), F32).at[0].set(c_ctx).at[1:1 + n_batch].set(c)
    mods_all = _modulation(cc, mod_w, mod_b)
    cos, sin = _rope_tables(lay)
    x_all = jnp.concatenate([ctx.reshape(n_batch * n_ctx, d), x.reshape(n_batch * seq, d)], axis=0)

    for l in range(depth):
        last = l == depth - 1
        lam_init = 0.8 - 0.6 * math.exp(-0.3 * l)
        mods = mods_all[l]

        proj = _in_projection(lay, x_all, norm_mix[l], mods, w_in[l].astype(BF16))
        hf = _rglru(lay, proj, rg_conv_w[l], rg_conv_b[l], rg_gate_w[l, 0], rg_gate_b[l, 0],
                    rg_lam[l, 0], reverse=False)
        ya = _rglru(lay, proj, rg_conv_w[l], rg_conv_b[l], rg_gate_w[l, 1], rg_gate_b[l, 1],
                    rg_lam[l, 1], reverse=True, hf=hf)
        yb, yc = _conv_pool(lay, proj, sc_conv_w[l], pool_w[l], pool_b[l], pool_scale[l])
        qs, kr, vt = _rope(lay, proj, cos, sin)
        yd = _attention(lay, qs, kr, vt, da_lam[l], da_norm[l], lam_init)
        x_all = _out_projection(lay, (ya, yb, yc, yd), w_out[l].astype(BF16), x_all, mods)

        first_tile = 1 if last else 0
        out = _moe(lay, x_all, norm_ffn[l], mods, moe_grp_w[l], moe_grp_b[l], moe_exp_w[l], moe_exp_b[l],
                   moe_up[l], moe_down[l], norm_final, first_tile, last)
        if last:
            return out.reshape(n_batch, seq, d)
        x_all = out
```

```python
import functools
import math

import jax
import jax.numpy as jnp
from jax import lax
from jax.experimental import pallas as pl
from jax.experimental.pallas import tpu as pltpu

F32 = jnp.float32
BF16 = jnp.bfloat16

W_GROUP = 512
N_PROJ = 9
RG_BLOCKS = 4
RG_BW = 128
RG_C = 8.0
POOL_WINDOWS = (2, 4, 8, 16)
POOL_GW = 128
DA_HEADS = 4
DA_HD = 64
DA_VD = 128
ROPE_THETA = 10000.0
GRID_W = 64
N_GROUPS = 4
EXP_PER_GROUP = 8
N_EXPERTS = 32
D_EXPERT = 512
N_MOD = 6
EPS = 1e-6

SUBLANES = 8
LANES = 128
ROW_TILE = 256
MM_TILE = 512
MAX_KV_CHUNK = 2816
ONES_ROWS = 16
EXP_TILE = 256
DMA_UNROLL = 8
ISSUE_PARTS = 8
VMEM_LIMIT = 56 * 1024 * 1024


def _cparams(n_axes):
    return pltpu.CompilerParams(dimension_semantics=("arbitrary",) * n_axes,
                                vmem_limit_bytes=VMEM_LIMIT)


def _mod_kernel(c_ref, w_ref, b_ref, o_ref):
    c = c_ref[...]
    s = c * jax.nn.sigmoid(c)
    o_ref[0] = jnp.dot(s.astype(BF16), w_ref[0].astype(BF16), preferred_element_type=F32) + b_ref[0]


def _modulation(cc, mod_w, mod_b):
    depth, d, n = mod_w.shape
    tn = 1024
    out = pl.pallas_call(
        _mod_kernel,
        grid=(depth, n // tn),
        in_specs=[pl.BlockSpec((SUBLANES, d), lambda l, j: (0, 0)),
                  pl.BlockSpec((1, d, tn), lambda l, j: (l, 0, j)),
                  pl.BlockSpec((1, 1, tn), lambda l, j: (l, 0, j))],
        out_specs=pl.BlockSpec((1, SUBLANES, tn), lambda l, j: (l, 0, j)),
        out_shape=jax.ShapeDtypeStruct((depth, SUBLANES, n), F32),
        compiler_params=_cparams(2),
        name="modulation",
    )(cc, mod_w, mod_b.reshape(depth, 1, n))
    return out.reshape(depth, SUBLANES, N_MOD, d)


def _rms_mod(x, g, shift, scale):
    r = lax.rsqrt(jnp.mean(x * x, axis=-1, keepdims=True) + EPS)
    return (x * r) * g * (1.0 + scale) + shift


class _Layout:
    def __init__(self, n_batch, n_ctx, seq):
        assert n_ctx == ROW_TILE, "one sequence tile per context sequence"
        assert n_batch * n_ctx == MM_TILE, "context rows fill exactly one projection tile"
        assert seq % MM_TILE == 0 and seq % GRID_W == 0
        self.b = n_batch
        self.n_ctx = n_ctx
        self.seq = seq
        self.rows = n_batch * (n_ctx + seq)
        self.nl = seq // ROW_TILE
        self.n_tiles = self.rows // ROW_TILE

    def mod_row_of_seq_tile(self, i):
        return jnp.where(i < self.b, 0, 1 + (i - self.b) // self.nl)

    def mod_row_of_mm_tile(self, i):
        return jnp.where(i < 1, 0, 1 + (i - 1) // (self.seq // MM_TILE))


def _inproj_kernel(x_ref, g_ref, m_ref, w_ref, o_ref, h_scr):
    @pl.when(pl.program_id(1) == 0)
    def _():
        h = _rms_mod(x_ref[...], g_ref[...], m_ref[0:1, :], m_ref[1:2, :])
        h_scr[...] = h.astype(BF16)

    o_ref[...] = jnp.dot(h_scr[...], w_ref[...], preferred_element_type=F32)


def _in_projection(lay, x_all, gain, mods, w_bf):
    t, d = x_all.shape
    n = w_bf.shape[1]
    tn = 1536
    return pl.pallas_call(
        _inproj_kernel,
        grid=(t // MM_TILE, n // tn),
        in_specs=[pl.BlockSpec((MM_TILE, d), lambda i, j: (i, 0)),
                  pl.BlockSpec((1, d), lambda i, j: (0, 0)),
                  pl.BlockSpec((None, N_MOD, d), lambda i, j: (lay.mod_row_of_mm_tile(i), 0, 0)),
                  pl.BlockSpec((d, tn), lambda i, j: (0, j))],
        out_specs=pl.BlockSpec((MM_TILE, tn), lambda i, j: (i, j)),
        out_shape=jax.ShapeDtypeStruct((t, n), F32),
        scratch_shapes=[pltpu.VMEM((MM_TILE, d), BF16)],
        compiler_params=_cparams(2),
        name="in_projection",
    )(x_all, gain.reshape(1, d), mods, w_bf)


def _fill_ext(ext, prev_ref, x, next_ref, is_first, is_last):
    ext[0:SUBLANES, :] = jnp.where(is_first, 0.0, prev_ref[...])
    ext[SUBLANES:SUBLANES + ROW_TILE, :] = x
    ext[SUBLANES + ROW_TILE:, :] = jnp.where(is_last, 0.0, next_ref[...])


def _shifted(ext, off, cols=slice(None)):
    return ext[SUBLANES + off:SUBLANES + off + ROW_TILE, cols]


def _rglru_kernel(reverse, n_lat_tiles, *refs):
    if reverse:
        (x_ref, xp_ref, xn_ref, cw_ref, cb_ref, gw_ref, gb_ref, lam_ref, hf_ref, ag_ref,
         o_ref, ext, a_scr, b_scr, carry) = refs
    else:
        (x_ref, xp_ref, xn_ref, cw_ref, cb_ref, gw_ref, gb_ref, lam_ref,
         o_ref, ext, a_scr, b_scr, carry) = refs
    s = pl.program_id(1)
    if reverse:
        is_first = (s == 0) | (s == n_lat_tiles)
        is_last = (s == 0) | (s == 1)
    else:
        is_first = (s == 0) | (s == 1)
        is_last = (s == 0) | (s == n_lat_tiles)

    @pl.when(s == 0)
    def _():
        carry[...] = jnp.zeros_like(carry)

    _fill_ext(ext, xp_ref, x_ref[...], xn_ref, is_first, is_last)
    u = cb_ref[...] + sum(cw_ref[k:k + 1, :] * _shifted(ext, k - 2) for k in range(4))

    sp = -lam_ref[...]
    sp = jnp.maximum(sp, 0.0) + jnp.log1p(jnp.exp(-jnp.abs(sp)))
    for n in range(RG_BLOCKS):
        cols = slice(n * RG_BW, (n + 1) * RG_BW)
        un = u[:, cols]
        g = jnp.dot(un.astype(BF16), gw_ref[n], preferred_element_type=F32)
        r = jax.nn.sigmoid(g[:, :RG_BW] + gb_ref[0:1, cols])
        gi = jax.nn.sigmoid(g[:, RG_BW:] + gb_ref[1:2, cols])
        log_a = -RG_C * r * sp[:, cols]
        a_scr[:, cols] = jnp.exp(log_a)
        th = jnp.tanh(log_a)
        b_scr[:, cols] = jnp.sqrt(-2.0 * th / (1.0 - th)) * (gi * un)

    row = lax.broadcasted_iota(jnp.int32, (SUBLANES, W_GROUP), 0)
    n_groups = ROW_TILE // SUBLANES

    def group(gidx, h_prev):
        g0 = (n_groups - 1 - gidx) if reverse else gidx
        start = pl.multiple_of(g0 * SUBLANES, SUBLANES)
        a8 = a_scr[pl.ds(start, SUBLANES), :]
        b8 = b_scr[pl.ds(start, SUBLANES), :]
        for sh in (1, 2, 4):
            rot = (SUBLANES - sh) if reverse else sh
            a_sh = pltpu.roll(a8, rot, 0)
            b_sh = pltpu.roll(b8, rot, 0)
            valid = (row < SUBLANES - sh) if reverse else (row >= sh)
            b8 = jnp.where(valid, a8 * b_sh + b8, b8)
            a8 = jnp.where(valid, a8 * a_sh, a8)
        h8 = a8 * h_prev + b8
        a_scr[pl.ds(start, SUBLANES), :] = h8
        edge = 0 if reverse else SUBLANES - 1
        return h8[edge:edge + 1, :]

    carry[...] = lax.fori_loop(0, n_groups, group, carry[...], unroll=4)

    h = a_scr[...]
    if reverse:
        o_ref[...] = (jax.nn.gelu(ag_ref[...]) * (hf_ref[...] + h)).astype(o_ref.dtype)
    else:
        o_ref[...] = h


def _rglru(lay, proj, conv_w, conv_b, gate_w, gate_b, lam, reverse, hf=None):
    t = proj.shape[0]
    nl = lay.nl
    nb8 = ROW_TILE // SUBLANES

    def tile(b, s):
        lat = (nl - s) if reverse else (s - 1)
        return jnp.where(s == 0, b, lay.b + b * nl + lat)

    main = lambda c: pl.BlockSpec((ROW_TILE, W_GROUP), lambda b, s: (tile(b, s), c))
    prev = pl.BlockSpec((SUBLANES, W_GROUP), lambda b, s: (jnp.maximum(tile(b, s) * nb8 - 1, 0), 0))
    nxt = pl.BlockSpec((SUBLANES, W_GROUP),
                       lambda b, s: (jnp.minimum((tile(b, s) + 1) * nb8, t // SUBLANES - 1), 0))
    const = lambda shape: pl.BlockSpec(shape, lambda b, s: (0,) * len(shape))
    gw = jnp.transpose(gate_w, (1, 2, 0, 3)).reshape(RG_BLOCKS, RG_BW, 2 * RG_BW).astype(BF16)
    gb = gate_b.reshape(2, W_GROUP)
    in_specs = [main(0), prev, nxt, const((4, W_GROUP)), const((1, W_GROUP)),
                const((RG_BLOCKS, RG_BW, 2 * RG_BW)), const((2, W_GROUP)), const((1, W_GROUP))]
    args = [proj, proj, proj, conv_w, conv_b.reshape(1, W_GROUP), gw, gb, lam.reshape(1, W_GROUP)]
    if reverse:
        in_specs += [pl.BlockSpec((ROW_TILE, W_GROUP), lambda b, s: (tile(b, s), 0)), main(1)]
        args += [hf, proj]
    return pl.pallas_call(
        functools.partial(_rglru_kernel, reverse, nl),
        grid=(lay.b, 1 + nl),
        in_specs=in_specs,
        out_specs=pl.BlockSpec((ROW_TILE, W_GROUP), lambda b, s: (tile(b, s), 0)),
        out_shape=jax.ShapeDtypeStruct((t, W_GROUP), BF16 if reverse else F32),
        scratch_shapes=[pltpu.VMEM((ROW_TILE + 2 * SUBLANES, W_GROUP), F32),
                        pltpu.VMEM((ROW_TILE, W_GROUP), F32),
                        pltpu.VMEM((ROW_TILE, W_GROUP), F32),
                        pltpu.VMEM((1, W_GROUP), F32)],
        compiler_params=_cparams(2),
        name="rglru_bwd" if reverse else "rglru_fwd",
    )(*args)


def _conv_pool_kernel(lay, bb_ref, bc_ref, bcp_ref, bcn_ref, bx_ref, bxp_ref, bxn_ref,
                      cx_ref, cxp_ref, cxn_ref, scw_ref, pw_ref, pb_ref, ps_ref,
                      yb_ref, yc_ref, ext):
    i = pl.program_id(0)
    j = jnp.maximum(i - lay.b, 0) % lay.nl
    is_ctx = i < lay.b
    is_first = is_ctx | (j == 0)
    is_last = is_ctx | (j == lay.nl - 1)

    ext[0:SUBLANES, :] = jnp.where(is_first, 0.0, bcp_ref[...] * bxp_ref[...])
    ext[SUBLANES:SUBLANES + ROW_TILE, :] = bc_ref[...] * bx_ref[...]
    ext[SUBLANES + ROW_TILE:, :] = jnp.where(is_last, 0.0, bcn_ref[...] * bxn_ref[...])
    conv = sum(scw_ref[k:k + 1, :] * _shifted(ext, k - 1) for k in range(3))
    yb_ref[...] = (bb_ref[...] * conv).astype(yb_ref.dtype)

    x = cx_ref[...]
    _fill_ext(ext, cxp_ref, x, cxn_ref, is_first, is_last)
    seq_len = jnp.where(is_ctx, lay.n_ctx, lay.seq)
    pos = jnp.where(is_ctx, 0, j * ROW_TILE) + lax.broadcasted_iota(jnp.int32, (ROW_TILE, 1), 0)
    for gi, win in enumerate(POOL_WINDOWS):
        cols = slice(gi * POOL_GW, (gi + 1) * POOL_GW)
        half = win // 2
        tot = sum(_shifted(ext, off, cols) for off in range(-half, half))
        cnt = jnp.minimum(pos + half, seq_len) - jnp.maximum(pos - half, 0)
        dlt = tot / cnt.astype(F32) - x[:, cols]
        y = jnp.dot(dlt.astype(BF16), pw_ref[gi].astype(BF16), preferred_element_type=F32)
        yc_ref[:, cols] = ((y + pb_ref[gi:gi + 1, :]) * ps_ref[:, cols]).astype(yc_ref.dtype)


def _conv_pool(lay, proj, sc_w, pool_w, pool_b, pool_scale):
    t = proj.shape[0]
    nb8 = ROW_TILE // SUBLANES
    main = lambda c: pl.BlockSpec((ROW_TILE, W_GROUP), lambda i: (i, c))
    prev = lambda c: pl.BlockSpec((SUBLANES, W_GROUP), lambda i: (jnp.maximum(i * nb8 - 1, 0), c))
    nxt = lambda c: pl.BlockSpec((SUBLANES, W_GROUP),
                                 lambda i: (jnp.minimum((i + 1) * nb8, t // SUBLANES - 1), c))
    const = lambda shape: pl.BlockSpec(shape, lambda i: (0,) * len(shape))
    n_pool = len(POOL_WINDOWS)
    out = jax.ShapeDtypeStruct((t, W_GROUP), BF16)
    return pl.pallas_call(
        functools.partial(_conv_pool_kernel, lay),
        grid=(t // ROW_TILE,),
        in_specs=[main(2), main(3), prev(3), nxt(3), main(4), prev(4), nxt(4),
                  main(5), prev(5), nxt(5), const((3, W_GROUP)),
                  const((n_pool, POOL_GW, POOL_GW)), const((n_pool, POOL_GW)), const((1, W_GROUP))],
        out_specs=[pl.BlockSpec((ROW_TILE, W_GROUP), lambda i: (i, 0))] * 2,
        out_shape=[out, out],
        scratch_shapes=[pltpu.VMEM((ROW_TILE + 2 * SUBLANES, W_GROUP), F32)],
        compiler_params=_cparams(1),
        name="conv_pool",
    )(*([proj] * 10), sc_w, pool_w, pool_b, pool_scale.reshape(1, W_GROUP))


def _rope_tables(lay):
    nf = DA_HD // 4
    inv = ROPE_THETA ** (-jnp.arange(nf, dtype=F32) / nf)
    pos = jnp.arange(lay.seq, dtype=jnp.int32)
    ang_r = (pos // GRID_W).astype(F32)[:, None] * inv
    ang_c = (pos % GRID_W).astype(F32)[:, None] * inv
    cos = jnp.concatenate([jnp.cos(ang_r)] * 2 + [jnp.cos(ang_c)] * 2, axis=-1)
    sin = jnp.concatenate([-jnp.sin(ang_r), jnp.sin(ang_r), -jnp.sin(ang_c), jnp.sin(ang_c)], axis=-1)
    cos = jnp.concatenate([jnp.ones((ROW_TILE, DA_HD), F32), cos], axis=0)
    sin = jnp.concatenate([jnp.zeros((ROW_TILE, DA_HD), F32), sin], axis=0)
    return jnp.tile(cos, (1, 2)), jnp.tile(sin, (1, 2))


def _rope_kernel(q_ref, k_ref, v_ref, cos_ref, sin_ref, qo_ref, ko_ref, vto_ref):
    cos = cos_ref[...]
    sin = sin_ref[...]
    nf = DA_HD // 4
    lane = lax.broadcasted_iota(jnp.int32, (ROW_TILE, LANES), 1)
    first_half = (lane & (2 * nf - 1)) < nf

    def rot(x):
        partner = jnp.where(first_half, pltpu.roll(x, LANES - nf, 1), pltpu.roll(x, nf, 1))
        return x * cos + partner * sin

    for h in range(DA_HEADS):
        cols = slice(h * DA_VD, (h + 1) * DA_VD)
        qo_ref[:, cols] = (rot(q_ref[:, cols]) * (DA_HD ** -0.5)).astype(BF16)
        ko_ref[:, cols] = rot(k_ref[:, cols]).astype(BF16)
        vto_ref[cols, :] = v_ref[:, cols].T.astype(BF16)


def _rope(lay, proj, cos, sin):
    t = proj.shape[0]
    nl = lay.nl
    main = lambda c: pl.BlockSpec((ROW_TILE, W_GROUP), lambda i: (i, c))
    tab = pl.BlockSpec((ROW_TILE, LANES), lambda i: (jnp.where(i < lay.b, 0, 1 + (i - lay.b) % nl), 0))

    def key_tile(i):
        j = jnp.maximum(i - lay.b, 0)
        return jnp.where(i < lay.b, i * (nl + 1), (j // nl) * (nl + 1) + 1 + j % nl)

    return pl.pallas_call(
        _rope_kernel,
        grid=(t // ROW_TILE,),
        in_specs=[main(6), main(7), main(8), tab, tab],
        out_specs=[pl.BlockSpec((ROW_TILE, W_GROUP), lambda i: (i, 0)),
                   pl.BlockSpec((ROW_TILE, W_GROUP), lambda i: (key_tile(i), 0)),
                   pl.BlockSpec((W_GROUP, ROW_TILE), lambda i: (0, key_tile(i)))],
        out_shape=[jax.ShapeDtypeStruct((t, W_GROUP), BF16),
                   jax.ShapeDtypeStruct((t, W_GROUP), BF16),
                   jax.ShapeDtypeStruct((W_GROUP, t), BF16)],
        compiler_params=_cparams(1),
        name="rope",
    )(proj, proj, proj, cos, sin)


def _kv_chunk(n_keys):
    return max(c for c in range(ROW_TILE, MAX_KV_CHUNK + 1, ROW_TILE) if n_keys % c == 0)


def _attn_kernel(lam_init, n_ctx, chunk, q_ref, k_ref, vt_ref, lp_ref, g_ref, o_ref, qqt, m_scr, acc):
    qi = pl.program_id(2)
    tq = q_ref.shape[0]
    n_keys = k_ref.shape[0]

    qt = q_ref[...].astype(F32).T
    feat = lax.broadcasted_iota(jnp.int32, qt.shape, 0)
    qqt[:, 0:tq] = jnp.where(feat < DA_HD, qt, 0.0).astype(BF16)
    qqt[:, tq:] = jnp.where(feat >= DA_HD, qt, 0.0).astype(BF16)
    m_scr[...] = jnp.full_like(m_scr, -jnp.inf)
    acc[...] = jnp.zeros_like(acc)

    def scores(start, size):
        s = jnp.dot(k_ref[start:start + size, :], qqt[...], preferred_element_type=F32)
        return s, jnp.max(s, axis=0, keepdims=True)

    def accumulate(start, size, s, s_max):
        m_prev = m_scr[0:1, :]
        m_new = jnp.maximum(m_prev, s_max)
        alpha = jnp.exp(m_prev - m_new)
        p = jnp.exp(s - m_new).astype(BF16)
        vt = jnp.concatenate([vt_ref[:, start:start + size], jnp.ones((ONES_ROWS, size), BF16)], axis=0)
        acc[...] = alpha * acc[...] + jnp.dot(vt, p, preferred_element_type=F32)
        m_scr[...] = jnp.broadcast_to(m_new, m_scr.shape)

    @pl.when(qi == 0)
    def _():
        accumulate(0, n_ctx, *scores(0, n_ctx))

    @pl.when(qi > 0)
    def _():
        n_chunks = n_keys // chunk
        cur = scores(0, chunk)
        for c in range(n_chunks):
            nxt = scores((c + 1) * chunk, chunk) if c + 1 < n_chunks else None
            accumulate(c * chunk, chunk, *cur)
            cur = nxt

    lp = lp_ref[...]
    lam = (jnp.exp(jnp.sum(lp[0:1] * lp[1:2], axis=-1, keepdims=True))
           - jnp.exp(jnp.sum(lp[2:3] * lp[3:4], axis=-1, keepdims=True)) + lam_init)
    den = acc[DA_VD:DA_VD + 1, :]
    ot = acc[0:DA_VD, 0:tq] / den[:, 0:tq] - lam * (acc[0:DA_VD, tq:] / den[:, tq:])
    o = ot.T
    r = lax.rsqrt(jnp.mean(o * o, axis=-1, keepdims=True) + EPS)
    o_ref[...] = ((o * r) * g_ref[...] * (1.0 - lam_init)).astype(o_ref.dtype)


def _attention(lay, qs, kr, vt, da_lam, da_norm, lam_init):
    t = qs.shape[0]
    nl = lay.nl
    tq = ROW_TILE
    n_keys = lay.n_ctx + lay.seq
    q_spec = pl.BlockSpec((tq, DA_VD), lambda b, h, qi: (jnp.where(qi == 0, b, lay.b + b * nl + qi - 1), h))
    const = lambda shape: pl.BlockSpec(shape, lambda b, h, qi: (0,) * len(shape))
    return pl.pallas_call(
        functools.partial(_attn_kernel, lam_init, lay.n_ctx, _kv_chunk(n_keys)),
        grid=(lay.b, DA_HEADS, 1 + nl),
        in_specs=[q_spec,
                  pl.BlockSpec((n_keys, DA_VD), lambda b, h, qi: (b, h)),
                  pl.BlockSpec((DA_VD, n_keys), lambda b, h, qi: (h, b)),
                  const((4, DA_HD)), const((1, DA_VD))],
        out_specs=q_spec,
        out_shape=jax.ShapeDtypeStruct((t, W_GROUP), BF16),
        scratch_shapes=[pltpu.VMEM((DA_VD, 2 * tq), BF16),
                        pltpu.VMEM((SUBLANES, 2 * tq), F32),
                        pltpu.VMEM((DA_VD + ONES_ROWS, 2 * tq), F32)],
        compiler_params=_cparams(3),
        name="attention",
    )(qs, kr, vt, da_lam, da_norm.reshape(1, DA_VD))


def _outproj_kernel(ya_ref, yb_ref, yc_ref, yd_ref, w_ref, x_ref, m_ref, o_ref):
    y = jnp.concatenate([ya_ref[...], yb_ref[...], yc_ref[...], yd_ref[...]], axis=-1)
    o_ref[...] = x_ref[...] + m_ref[2:3, :] * jnp.dot(y, w_ref[...], preferred_element_type=F32)


def _out_projection(lay, ys, w_bf, x_all, mods):
    t, d = x_all.shape
    tn = 1024
    y_spec = pl.BlockSpec((MM_TILE, W_GROUP), lambda j, i: (i, 0))
    return pl.pallas_call(
        _outproj_kernel,
        grid=(d // tn, t // MM_TILE),
        in_specs=[y_spec] * 4 + [
            pl.BlockSpec((d, tn), lambda j, i: (0, j)),
            pl.BlockSpec((MM_TILE, tn), lambda j, i: (i, j)),
            pl.BlockSpec((None, N_MOD, tn), lambda j, i: (lay.mod_row_of_mm_tile(i), 0, j))],
        out_specs=pl.BlockSpec((MM_TILE, tn), lambda j, i: (i, j)),
        out_shape=jax.ShapeDtypeStruct((t, d), F32),
        compiler_params=_cparams(2),
        name="out_projection",
    )(*ys, w_bf, x_all, mods)


R_E1, R_E2, R_W1, R_W2, R_RANK1, R_RANK2 = range(6)


def _router_kernel(x_ref, g_ref, m_ref, w_ref, b_ref, fl_ref, rec_ref, cnt_ref, counts):
    i = pl.program_id(0)

    @pl.when(i == 0)
    def _():
        counts[...] = jnp.zeros_like(counts)

    fl = _rms_mod(x_ref[...], g_ref[...], m_ref[3:4, :], m_ref[4:5, :])
    fl_ref[...] = fl
    logit = jnp.dot(fl, w_ref[...], preferred_element_type=F32, precision=lax.Precision.HIGHEST) + b_ref[...]
    tm = logit.shape[0]
    lane = lax.broadcasted_iota(jnp.int32, (tm, LANES), 1).astype(F32)
    neg = -jnp.inf

    def first_argmax(vals):
        top = jnp.max(vals, axis=-1, keepdims=True)
        return top, jnp.min(jnp.where(vals == top, lane, float(LANES)), axis=-1, keepdims=True)

    is_grp = (lane >= N_EXPERTS) & (lane < N_EXPERTS + N_GROUPS)
    g_top, g_lane = first_argmax(jnp.where(is_grp, logit, neg))
    g_w = 1.0 / jnp.sum(jnp.where(is_grp, jnp.exp(logit - g_top), 0.0), axis=-1, keepdims=True)

    grp_start = (g_lane - N_EXPERTS) * EXP_PER_GROUP
    in_grp = (lane >= grp_start) & (lane < grp_start + EXP_PER_GROUP)
    v1, e1 = first_argmax(jnp.where(in_grp, logit, neg))
    v2, e2 = first_argmax(jnp.where(in_grp & (lane != e1), logit, neg))
    z = jnp.exp(v2 - v1)
    w1 = g_w / (1.0 + z)
    w2 = g_w * z / (1.0 + z)

    oh1 = lane == e1
    oh2 = lane == e2
    oh = jnp.where(oh1 | oh2, 1.0, 0.0).astype(BF16)
    r_i = lax.broadcasted_iota(jnp.int32, (tm, tm), 0)
    c_i = lax.broadcasted_iota(jnp.int32, (tm, tm), 1)
    below = jnp.where(c_i < r_i, 1.0, 0.0).astype(BF16)
    before = counts[0:1, :] + jnp.dot(below, oh, preferred_element_type=F32)
    rank1 = jnp.sum(jnp.where(oh1, before, 0.0), axis=-1, keepdims=True)
    rank2 = jnp.sum(jnp.where(oh2, before, 0.0), axis=-1, keepdims=True)
    counts[...] = counts[...] + jnp.sum(oh.astype(F32), axis=0, keepdims=True)
    cnt_ref[...] = counts[...]

    rec = jnp.zeros((tm, LANES), F32)
    for slot, val in ((R_E1, e1), (R_E2, e2), (R_W1, w1), (R_W2, w2), (R_RANK1, rank1), (R_RANK2, rank2)):
        rec = jnp.where(lane == slot, val, rec)
    rec_ref[...] = rec


def _router(lay, x_all, gain, mods, w_exp, b_exp, w_grp, b_grp, first_tile):
    t, d = x_all.shape
    n_rows = t - first_tile * MM_TILE
    pad = LANES - N_EXPERTS - N_GROUPS
    w = jnp.concatenate([w_exp, w_grp, jnp.zeros((d, pad), F32)], axis=1)
    b = jnp.concatenate([b_exp, b_grp, jnp.zeros((pad,), F32)]).reshape(1, LANES)
    return pl.pallas_call(
        _router_kernel,
        grid=(n_rows // MM_TILE,),
        in_specs=[pl.BlockSpec((MM_TILE, d), lambda i: (i + first_tile, 0)),
                  pl.BlockSpec((1, d), lambda i: (0, 0)),
                  pl.BlockSpec((None, N_MOD, d), lambda i: (lay.mod_row_of_mm_tile(i + first_tile), 0, 0)),
                  pl.BlockSpec((d, LANES), lambda i: (0, 0)),
                  pl.BlockSpec((1, LANES), lambda i: (0, 0))],
        out_specs=[pl.BlockSpec((MM_TILE, d), lambda i: (i, 0)),
                   pl.BlockSpec((MM_TILE, LANES), lambda i: (i, 0)),
                   pl.BlockSpec((SUBLANES, LANES), lambda i: (0, 0))],
        out_shape=[jax.ShapeDtypeStruct((n_rows, d), F32),
                   jax.ShapeDtypeStruct((n_rows, LANES), F32),
                   jax.ShapeDtypeStruct((SUBLANES, LANES), F32)],
        scratch_shapes=[pltpu.VMEM((SUBLANES, LANES), F32)],
        compiler_params=_cparams(1),
        name="moe_router",
    )(x_all, gain.reshape(1, d), mods, w, b)


def _row_copy(src_hbm, src_row, buf, dst_row, sem):
    return pltpu.make_async_copy(src_hbm.at[pl.ds(src_row, 1), :], buf.at[pl.ds(dst_row, 1), :], sem)


def _expert_kernel(te_ref, src_ref, nt_ref, fl_hbm, wup_ref, wdn_ref, o_ref,
                   xbuf, sems, wup_bf, wdn_bf):
    i = pl.program_id(0)
    nt = nt_ref[0]
    slot = i % 2

    def start_gather(tile, slot_):
        def body(r, c):
            _row_copy(fl_hbm, src_ref[tile * EXP_TILE + r], xbuf.at[slot_], r, sems.at[slot_]).start()
            return c
        lax.fori_loop(0, EXP_TILE, body, 0, unroll=DMA_UNROLL)

    def wait_gather(slot_):
        for r in range(EXP_TILE):
            _row_copy(fl_hbm, 0, xbuf.at[slot_], r, sems.at[slot_]).wait()

    @pl.when(i == 0)
    def _():
        start_gather(0, 0)

    @pl.when(i < nt)
    def _():
        expert_changed = (i == 0) | (te_ref[i] != te_ref[jnp.maximum(i - 1, 0)])

        @pl.when(expert_changed)
        def _():
            wup_bf[...] = wup_ref[0].astype(BF16)
            wdn_bf[...] = wdn_ref[0].astype(BF16)

        wait_gather(slot)
        x = xbuf[slot].astype(BF16)
        hu = jnp.dot(x, wup_bf[...], preferred_element_type=F32)
        h = (jax.nn.silu(hu[:, :D_EXPERT]) * hu[:, D_EXPERT:]).astype(BF16)
        part = o_ref.shape[1] // ISSUE_PARTS
        rows = EXP_TILE // ISSUE_PARTS
        for c in range(ISSUE_PARTS):
            cols = slice(c * part, (c + 1) * part)
            o_ref[:, cols] = jnp.dot(h, wdn_bf[:, cols], preferred_element_type=F32)
            for r in range(c * rows, (c + 1) * rows):
                _row_copy(fl_hbm, src_ref[(i + 1) * EXP_TILE + r], xbuf.at[1 - slot], r,
                          sems.at[1 - slot]).start()

    @pl.when(i == nt)
    def _():
        wait_gather(slot)

    @pl.when(i >= nt)
    def _():
        o_ref[...] = jnp.zeros_like(o_ref)


def _experts(fl, tile_expert, src_rows, n_tiles_used, w_up, w_down, n_tiles):
    d = fl.shape[1]
    last = lambda i, nt: jnp.minimum(i, nt[0] - 1)
    grid_spec = pltpu.PrefetchScalarGridSpec(
        num_scalar_prefetch=3,
        grid=(n_tiles,),
        in_specs=[pl.BlockSpec(memory_space=pl.ANY),
                  pl.BlockSpec((1, d, 2 * D_EXPERT), lambda i, te, src, nt: (te[last(i, nt)], 0, 0)),
                  pl.BlockSpec((1, D_EXPERT, d), lambda i, te, src, nt: (te[last(i, nt)], 0, 0))],
        out_specs=pl.BlockSpec((EXP_TILE, d), lambda i, te, src, nt: (i, 0)),
        scratch_shapes=[pltpu.VMEM((2, EXP_TILE, d), F32),
                        pltpu.SemaphoreType.DMA((2,)),
                        pltpu.VMEM((d, 2 * D_EXPERT), BF16),
                        pltpu.VMEM((D_EXPERT, d), BF16)])
    return pl.pallas_call(
        _expert_kernel,
        grid_spec=grid_spec,
        out_shape=jax.ShapeDtypeStruct((n_tiles * EXP_TILE, d), F32),
        compiler_params=_cparams(1),
        name="moe_experts",
    )(tile_expert, src_rows, n_tiles_used, fl, w_up, w_down)


def _combine_kernel(final, pos_ref, ys_hbm, x_ref, rec_ref, m_ref, gf_ref, o_ref, ybuf, sems):
    i = pl.program_id(0)
    n = pl.num_programs(0)
    slot = i % 2

    def start_gather(tile, slot_):
        def body(r, c):
            for k in range(2):
                _row_copy(ys_hbm, pos_ref[(tile * ROW_TILE + r) * 2 + k], ybuf.at[slot_, k], r,
                          sems.at[slot_]).start()
            return c
        lax.fori_loop(0, ROW_TILE, body, 0, unroll=DMA_UNROLL)

    def wait_gather(slot_):
        for r in range(2 * ROW_TILE):
            _row_copy(ys_hbm, 0, ybuf.at[slot_, r % 2], r // 2, sems.at[slot_]).wait()

    @pl.when(i == 0)
    def _():
        start_gather(0, 0)

    wait_gather(slot)
    rec = rec_ref[...]
    w1 = rec[:, R_W1:R_W1 + 1]
    w2 = rec[:, R_W2:R_W2 + 1]
    part = o_ref.shape[1] // ISSUE_PARTS
    toks = ROW_TILE // ISSUE_PARTS
    for c in range(ISSUE_PARTS):
        cols = slice(c * part, (c + 1) * part)
        y = w1 * ybuf[slot, 0, :, cols] + w2 * ybuf[slot, 1, :, cols]
        o_ref[:, cols] = x_ref[:, cols] + m_ref[5:6, cols] * y
        for r in range(c * toks, (c + 1) * toks):
            for k in range(2):
                _row_copy(ys_hbm, pos_ref[((i + 1) * ROW_TILE + r) * 2 + k], ybuf.at[1 - slot, k], r,
                          sems.at[1 - slot]).start()
    if final:
        x = o_ref[...]
        r = lax.rsqrt(jnp.mean(x * x, axis=-1, keepdims=True) + EPS)
        o_ref[...] = (x * r) * gf_ref[...]

    @pl.when(i == n - 1)
    def _():
        wait_gather(1 - slot)


def _combine(lay, pos, ys, x_all, rec, mods, gain_final, first_tile, final):
    d = x_all.shape[1]
    n_rows = rec.shape[0]
    off = first_tile * (MM_TILE // ROW_TILE)
    grid_spec = pltpu.PrefetchScalarGridSpec(
        num_scalar_prefetch=1,
        grid=(n_rows // ROW_TILE,),
        in_specs=[pl.BlockSpec(memory_space=pl.ANY),
                  pl.BlockSpec((ROW_TILE, d), lambda i, pos: (i + off, 0)),
                  pl.BlockSpec((ROW_TILE, LANES), lambda i, pos: (i, 0)),
                  pl.BlockSpec((None, N_MOD, d), lambda i, pos: (lay.mod_row_of_seq_tile(i + off), 0, 0)),
                  pl.BlockSpec((1, d), lambda i, pos: (0, 0))],
        out_specs=pl.BlockSpec((ROW_TILE, d), lambda i, pos: (i, 0)),
        scratch_shapes=[pltpu.VMEM((2, 2, ROW_TILE, d), F32),
                        pltpu.SemaphoreType.DMA((2,))])
    return pl.pallas_call(
        functools.partial(_combine_kernel, final),
        grid_spec=grid_spec,
        out_shape=jax.ShapeDtypeStruct((n_rows, d), F32),
        compiler_params=_cparams(1),
        name="moe_combine",
    )(jnp.concatenate([pos, jnp.zeros((2 * ROW_TILE,), jnp.int32)]), ys, x_all, rec, mods,
      gain_final.reshape(1, d))


def _dispatch_plan(rec, counts, n_tiles):
    n_rows = rec.shape[0]
    cnt = counts[0, :N_EXPERTS].astype(jnp.int32)
    tiles_per = (cnt + EXP_TILE - 1) // EXP_TILE
    tile_end = jnp.cumsum(tiles_per)
    offset = (tile_end - tiles_per) * EXP_TILE
    e_sel = rec[:, R_E1:R_E2 + 1].astype(jnp.int32)
    rank = rec[:, R_RANK1:R_RANK2 + 1].astype(jnp.int32)
    pos = (offset[e_sel] + rank).reshape(-1)
    token = jnp.repeat(jnp.arange(n_rows, dtype=jnp.int32), 2)
    src_rows = jnp.zeros((n_tiles * EXP_TILE,), jnp.int32).at[pos].set(token)
    n_used = tile_end[-1]
    tile_ids = jnp.minimum(jnp.arange(n_tiles, dtype=jnp.int32), n_used - 1)
    tile_expert = jnp.sum(tile_end[None, :] <= tile_ids[:, None], axis=1).astype(jnp.int32)
    return pos, src_rows, tile_expert, n_used.reshape(1).astype(jnp.int32)


def _moe(lay, x_all, gain, mods, w_grp, b_grp, w_exp, b_exp, w_up_all, w_down_all, layer, gain_final,
         first_tile, final):
    fl, rec, counts = _router(lay, x_all, gain, mods, w_exp, b_exp, w_grp, b_grp, first_tile)
    n_rows = fl.shape[0]
    assert (2 * n_rows) % EXP_TILE == 0
    n_tiles = (2 * n_rows) // EXP_TILE + N_EXPERTS
    pos, src_rows, tile_expert, n_used = _dispatch_plan(rec, counts, n_tiles)
    d = fl.shape[1]
    w_up = w_up_all.reshape(-1, d, 2 * D_EXPERT)
    w_down = w_down_all.reshape(-1, D_EXPERT, d)
    ys = _experts(fl, tile_expert + layer * N_EXPERTS, src_rows, n_used, w_up, w_down, n_tiles)
    return _combine(lay, pos, ys, x_all, rec, mods, gain_final, first_tile, final)


def kernel(x, c, ctx, c_ctx, mod_w, mod_b, norm_mix, norm_ffn, w_in, w_out, rg_conv_w, rg_conv_b,
           rg_gate_w, rg_gate_b, rg_lam, sc_conv_w, pool_w, pool_b, pool_scale, da_lam, da_norm,
           moe_grp_w, moe_grp_b, moe_exp_w, moe_exp_b, moe_up, moe_down, norm_final):
    n_batch, seq, d = x.shape
    n_ctx = ctx.shape[1]
    depth = mod_w.shape[0]
    lay = _Layout(n_batch, n_ctx, seq)
    assert 1 + n_batch <= SUBLANES

    cc = jnp.zeros((SUBLANES, d), F32).at[0].set(c_ctx).at[1:1 + n_batch].set(c)
    mods_all = _modulation(cc, mod_w, mod_b)
    cos, sin = _rope_tables(lay)
    x_all = jnp.concatenate([ctx.reshape(n_batch * n_ctx, d), x.reshape(n_batch * seq, d)], axis=0)

    for l in range(depth):
        last = l == depth - 1
        lam_init = 0.8 - 0.6 * math.exp(-0.3 * l)
        mods = mods_all[l]

        proj = _in_projection(lay, x_all, norm_mix[l], mods, w_in[l].astype(BF16))
        hf = _rglru(lay, proj, rg_conv_w[l], rg_conv_b[l], rg_gate_w[l, 0], rg_gate_b[l, 0],
                    rg_lam[l, 0], reverse=False)
        ya = _rglru(lay, proj, rg_conv_w[l], rg_conv_b[l], rg_gate_w[l, 1], rg_gate_b[l, 1],
                    rg_lam[l, 1], reverse=True, hf=hf)
        yb, yc = _conv_pool(lay, proj, sc_conv_w[l], pool_w[l], pool_b[l], pool_scale[l])
        qs, kr, vt = _rope(lay, proj, cos, sin)
        yd = _attention(lay, qs, kr, vt, da_lam[l], da_norm[l], lam_init)
        x_all = _out_projection(lay, (ya, yb, yc, yd), w_out[l].astype(BF16), x_all, mods)

        first_tile = 1 if last else 0
        out = _moe(lay, x_all, norm_ffn[l], mods, moe_grp_w[l], moe_grp_b[l], moe_exp_w[l], moe_exp_b[l],
                   moe_up, moe_down, l, norm_final, first_tile, last)
        if last:
            return out.reshape(n_batch, seq, d)
        x_all = out
```

```python
import functools
import math

import jax
import jax.numpy as jnp
from jax import lax
from jax.experimental import pallas as pl
from jax.experimental.pallas import tpu as pltpu

F32 = jnp.float32
BF16 = jnp.bfloat16

W_GROUP = 512
N_PROJ = 9
RG_BLOCKS = 4
RG_BW = 128
RG_C = 8.0
POOL_WINDOWS = (2, 4, 8, 16)
POOL_GW = 128
DA_HEADS = 4
DA_HD = 64
DA_VD = 128
ROPE_THETA = 10000.0
GRID_W = 64
N_GROUPS = 4
EXP_PER_GROUP = 8
N_EXPERTS = 32
D_EXPERT = 512
N_MOD = 6
EPS = 1e-6

SUBLANES = 8
LANES = 128
ROW_TILE = 256
MM_TILE = 512
MAX_KV_CHUNK = 2816
ONES_ROWS = 16
EXP_TILE = 256
DMA_UNROLL = 8
ISSUE_PARTS = 8
ROW_CHUNKS = 16
ROW_PITCH = 24
VMEM_LIMIT = 56 * 1024 * 1024


def _cparams(n_axes):
    return pltpu.CompilerParams(dimension_semantics=("arbitrary",) * n_axes,
                                vmem_limit_bytes=VMEM_LIMIT)


def _mod_kernel(c_ref, w_ref, b_ref, o_ref):
    c = c_ref[...]
    s = c * jax.nn.sigmoid(c)
    o_ref[0] = jnp.dot(s.astype(BF16), w_ref[0].astype(BF16), preferred_element_type=F32) + b_ref[0]


def _modulation(cc, mod_w, mod_b):
    depth, d, n = mod_w.shape
    tn = 1024
    out = pl.pallas_call(
        _mod_kernel,
        grid=(depth, n // tn),
        in_specs=[pl.BlockSpec((SUBLANES, d), lambda l, j: (0, 0)),
                  pl.BlockSpec((1, d, tn), lambda l, j: (l, 0, j)),
                  pl.BlockSpec((1, 1, tn), lambda l, j: (l, 0, j))],
        out_specs=pl.BlockSpec((1, SUBLANES, tn), lambda l, j: (l, 0, j)),
        out_shape=jax.ShapeDtypeStruct((depth, SUBLANES, n), F32),
        compiler_params=_cparams(2),
        name="modulation",
    )(cc, mod_w, mod_b.reshape(depth, 1, n))
    return out.reshape(depth, SUBLANES, N_MOD, d)


def _rms_mod(x, g, shift, scale):
    r = lax.rsqrt(jnp.mean(x * x, axis=-1, keepdims=True) + EPS)
    return (x * r) * g * (1.0 + scale) + shift


class _Layout:
    def __init__(self, n_batch, n_ctx, seq):
        assert n_ctx == ROW_TILE, "one sequence tile per context sequence"
        assert n_batch * n_ctx == MM_TILE, "context rows fill exactly one projection tile"
        assert seq % MM_TILE == 0 and seq % GRID_W == 0
        self.b = n_batch
        self.n_ctx = n_ctx
        self.seq = seq
        self.rows = n_batch * (n_ctx + seq)
        self.nl = seq // ROW_TILE
        self.n_tiles = self.rows // ROW_TILE

    def mod_row_of_seq_tile(self, i):
        return jnp.where(i < self.b, 0, 1 + (i - self.b) // self.nl)

    def mod_row_of_mm_tile(self, i):
        return jnp.where(i < 1, 0, 1 + (i - 1) // (self.seq // MM_TILE))


def _inproj_kernel(x_ref, g_ref, m_ref, w_ref, o_ref):
    h = _rms_mod(x_ref[...], g_ref[...], m_ref[0:1, :], m_ref[1:2, :])
    o_ref[...] = jnp.dot(h.astype(BF16), w_ref[...], preferred_element_type=F32)


def _in_projection(lay, x_all, gain, mods, w_bf):
    t, d = x_all.shape
    n = w_bf.shape[1]
    tn = n // 2
    return pl.pallas_call(
        _inproj_kernel,
        grid=(n // tn, t // MM_TILE),
        in_specs=[pl.BlockSpec((MM_TILE, d), lambda j, i: (i, 0)),
                  pl.BlockSpec((1, d), lambda j, i: (0, 0)),
                  pl.BlockSpec((None, N_MOD, d), lambda j, i: (lay.mod_row_of_mm_tile(i), 0, 0)),
                  pl.BlockSpec((d, tn), lambda j, i: (0, j))],
        out_specs=pl.BlockSpec((MM_TILE, tn), lambda j, i: (i, j)),
        out_shape=jax.ShapeDtypeStruct((t, n), F32),
        compiler_params=_cparams(2),
        name="in_projection",
    )(x_all, gain.reshape(1, d), mods, w_bf)


def _fill_ext(ext, prev_ref, x, next_ref, is_first, is_last):
    ext[0:SUBLANES, :] = jnp.where(is_first, 0.0, prev_ref[...])
    ext[SUBLANES:SUBLANES + ROW_TILE, :] = x
    ext[SUBLANES + ROW_TILE:, :] = jnp.where(is_last, 0.0, next_ref[...])


def _shifted(ext, off, cols=slice(None)):
    return ext[SUBLANES + off:SUBLANES + off + ROW_TILE, cols]


def _rglru_kernel(reverse, n_lat_tiles, *refs):
    if reverse:
        (x_ref, xp_ref, xn_ref, cw_ref, cb_ref, gw_ref, gb_ref, lam_ref, hf_ref, ag_ref,
         o_ref, ext, a_scr, b_scr, carry) = refs
    else:
        (x_ref, xp_ref, xn_ref, cw_ref, cb_ref, gw_ref, gb_ref, lam_ref,
         o_ref, ext, a_scr, b_scr, carry) = refs
    s = pl.program_id(1)
    if reverse:
        is_first = (s == 0) | (s == n_lat_tiles)
        is_last = (s == 0) | (s == 1)
    else:
        is_first = (s == 0) | (s == 1)
        is_last = (s == 0) | (s == n_lat_tiles)

    @pl.when(s == 0)
    def _():
        carry[...] = jnp.zeros_like(carry)

    _fill_ext(ext, xp_ref, x_ref[...], xn_ref, is_first, is_last)
    u = cb_ref[...] + sum(cw_ref[k:k + 1, :] * _shifted(ext, k - 2) for k in range(4))

    sp = -lam_ref[...]
    sp = jnp.maximum(sp, 0.0) + jnp.log1p(jnp.exp(-jnp.abs(sp)))
    for n in range(RG_BLOCKS):
        cols = slice(n * RG_BW, (n + 1) * RG_BW)
        un = u[:, cols]
        g = jnp.dot(un.astype(BF16), gw_ref[n], preferred_element_type=F32)
        r = jax.nn.sigmoid(g[:, :RG_BW] + gb_ref[0:1, cols])
        gi = jax.nn.sigmoid(g[:, RG_BW:] + gb_ref[1:2, cols])
        log_a = -RG_C * r * sp[:, cols]
        a_scr[:, cols] = jnp.exp(log_a)
        th = jnp.tanh(log_a)
        b_scr[:, cols] = jnp.sqrt(-2.0 * th / (1.0 - th)) * (gi * un)

    row = lax.broadcasted_iota(jnp.int32, (SUBLANES, W_GROUP), 0)
    n_groups = ROW_TILE // SUBLANES

    def group(gidx, h_prev):
        g0 = (n_groups - 1 - gidx) if reverse else gidx
        start = pl.multiple_of(g0 * SUBLANES, SUBLANES)
        a8 = a_scr[pl.ds(start, SUBLANES), :]
        b8 = b_scr[pl.ds(start, SUBLANES), :]
        for sh in (1, 2, 4):
            rot = (SUBLANES - sh) if reverse else sh
            a_sh = pltpu.roll(a8, rot, 0)
            b_sh = pltpu.roll(b8, rot, 0)
            valid = (row < SUBLANES - sh) if reverse else (row >= sh)
            b8 = jnp.where(valid, a8 * b_sh + b8, b8)
            a8 = jnp.where(valid, a8 * a_sh, a8)
        h8 = a8 * h_prev + b8
        a_scr[pl.ds(start, SUBLANES), :] = h8
        edge = 0 if reverse else SUBLANES - 1
        return h8[edge:edge + 1, :]

    carry[...] = lax.fori_loop(0, n_groups, group, carry[...], unroll=4)

    h = a_scr[...]
    if reverse:
        o_ref[...] = (jax.nn.gelu(ag_ref[...]) * (hf_ref[...] + h)).astype(o_ref.dtype)
    else:
        o_ref[...] = h


def _rglru(lay, proj, conv_w, conv_b, gate_w, gate_b, lam, reverse, hf=None):
    t = proj.shape[0]
    nl = lay.nl
    nb8 = ROW_TILE // SUBLANES

    def tile(b, s):
        lat = (nl - s) if reverse else (s - 1)
        return jnp.where(s == 0, b, lay.b + b * nl + lat)

    main = lambda c: pl.BlockSpec((ROW_TILE, W_GROUP), lambda b, s: (tile(b, s), c))
    prev = pl.BlockSpec((SUBLANES, W_GROUP), lambda b, s: (jnp.maximum(tile(b, s) * nb8 - 1, 0), 0))
    nxt = pl.BlockSpec((SUBLANES, W_GROUP),
                       lambda b, s: (jnp.minimum((tile(b, s) + 1) * nb8, t // SUBLANES - 1), 0))
    const = lambda shape: pl.BlockSpec(shape, lambda b, s: (0,) * len(shape))
    gw = jnp.transpose(gate_w, (1, 2, 0, 3)).reshape(RG_BLOCKS, RG_BW, 2 * RG_BW).astype(BF16)
    gb = gate_b.reshape(2, W_GROUP)
    in_specs = [main(0), prev, nxt, const((4, W_GROUP)), const((1, W_GROUP)),
                const((RG_BLOCKS, RG_BW, 2 * RG_BW)), const((2, W_GROUP)), const((1, W_GROUP))]
    args = [proj, proj, proj, conv_w, conv_b.reshape(1, W_GROUP), gw, gb, lam.reshape(1, W_GROUP)]
    if reverse:
        in_specs += [pl.BlockSpec((ROW_TILE, W_GROUP), lambda b, s: (tile(b, s), 0)), main(1)]
        args += [hf, proj]
    return pl.pallas_call(
        functools.partial(_rglru_kernel, reverse, nl),
        grid=(lay.b, 1 + nl),
        in_specs=in_specs,
        out_specs=pl.BlockSpec((ROW_TILE, W_GROUP), lambda b, s: (tile(b, s), 0)),
        out_shape=jax.ShapeDtypeStruct((t, W_GROUP), BF16 if reverse else F32),
        scratch_shapes=[pltpu.VMEM((ROW_TILE + 2 * SUBLANES, W_GROUP), F32),
                        pltpu.VMEM((ROW_TILE, W_GROUP), F32),
                        pltpu.VMEM((ROW_TILE, W_GROUP), F32),
                        pltpu.VMEM((1, W_GROUP), F32)],
        compiler_params=_cparams(2),
        name="rglru_bwd" if reverse else "rglru_fwd",
    )(*args)


def _conv_pool_kernel(lay, bb_ref, bc_ref, bcp_ref, bcn_ref, bx_ref, bxp_ref, bxn_ref,
                      cx_ref, cxp_ref, cxn_ref, scw_ref, pw_ref, pb_ref, ps_ref,
                      yb_ref, yc_ref, ext):
    i = pl.program_id(0)
    j = jnp.maximum(i - lay.b, 0) % lay.nl
    is_ctx = i < lay.b
    is_first = is_ctx | (j == 0)
    is_last = is_ctx | (j == lay.nl - 1)

    ext[0:SUBLANES, :] = jnp.where(is_first, 0.0, bcp_ref[...] * bxp_ref[...])
    ext[SUBLANES:SUBLANES + ROW_TILE, :] = bc_ref[...] * bx_ref[...]
    ext[SUBLANES + ROW_TILE:, :] = jnp.where(is_last, 0.0, bcn_ref[...] * bxn_ref[...])
    conv = sum(scw_ref[k:k + 1, :] * _shifted(ext, k - 1) for k in range(3))
    yb_ref[...] = (bb_ref[...] * conv).astype(yb_ref.dtype)

    x = cx_ref[...]
    _fill_ext(ext, cxp_ref, x, cxn_ref, is_first, is_last)
    seq_len = jnp.where(is_ctx, lay.n_ctx, lay.seq)
    pos = jnp.where(is_ctx, 0, j * ROW_TILE) + lax.broadcasted_iota(jnp.int32, (ROW_TILE, 1), 0)
    for gi, win in enumerate(POOL_WINDOWS):
        cols = slice(gi * POOL_GW, (gi + 1) * POOL_GW)
        half = win // 2
        tot = sum(_shifted(ext, off, cols) for off in range(-half, half))
        cnt = jnp.minimum(pos + half, seq_len) - jnp.maximum(pos - half, 0)
        dlt = tot / cnt.astype(F32) - x[:, cols]
        y = jnp.dot(dlt.astype(BF16), pw_ref[gi].astype(BF16), preferred_element_type=F32)
        yc_ref[:, cols] = ((y + pb_ref[gi:gi + 1, :]) * ps_ref[:, cols]).astype(yc_ref.dtype)


def _conv_pool(lay, proj, sc_w, pool_w, pool_b, pool_scale):
    t = proj.shape[0]
    nb8 = ROW_TILE // SUBLANES
    main = lambda c: pl.BlockSpec((ROW_TILE, W_GROUP), lambda i: (i, c))
    prev = lambda c: pl.BlockSpec((SUBLANES, W_GROUP), lambda i: (jnp.maximum(i * nb8 - 1, 0), c))
    nxt = lambda c: pl.BlockSpec((SUBLANES, W_GROUP),
                                 lambda i: (jnp.minimum((i + 1) * nb8, t // SUBLANES - 1), c))
    const = lambda shape: pl.BlockSpec(shape, lambda i: (0,) * len(shape))
    n_pool = len(POOL_WINDOWS)
    out = jax.ShapeDtypeStruct((t, W_GROUP), BF16)
    return pl.pallas_call(
        functools.partial(_conv_pool_kernel, lay),
        grid=(t // ROW_TILE,),
        in_specs=[main(2), main(3), prev(3), nxt(3), main(4), prev(4), nxt(4),
                  main(5), prev(5), nxt(5), const((3, W_GROUP)),
                  const((n_pool, POOL_GW, POOL_GW)), const((n_pool, POOL_GW)), const((1, W_GROUP))],
        out_specs=[pl.BlockSpec((ROW_TILE, W_GROUP), lambda i: (i, 0))] * 2,
        out_shape=[out, out],
        scratch_shapes=[pltpu.VMEM((ROW_TILE + 2 * SUBLANES, W_GROUP), F32)],
        compiler_params=_cparams(1),
        name="conv_pool",
    )(*([proj] * 10), sc_w, pool_w, pool_b, pool_scale.reshape(1, W_GROUP))


def _rope_tables(lay):
    nf = DA_HD // 4
    inv = ROPE_THETA ** (-jnp.arange(nf, dtype=F32) / nf)
    pos = jnp.arange(lay.seq, dtype=jnp.int32)
    ang_r = (pos // GRID_W).astype(F32)[:, None] * inv
    ang_c = (pos % GRID_W).astype(F32)[:, None] * inv
    cos = jnp.concatenate([jnp.cos(ang_r)] * 2 + [jnp.cos(ang_c)] * 2, axis=-1)
    sin = jnp.concatenate([-jnp.sin(ang_r), jnp.sin(ang_r), -jnp.sin(ang_c), jnp.sin(ang_c)], axis=-1)
    cos = jnp.concatenate([jnp.ones((ROW_TILE, DA_HD), F32), cos], axis=0)
    sin = jnp.concatenate([jnp.zeros((ROW_TILE, DA_HD), F32), sin], axis=0)
    return jnp.tile(cos, (1, 2)), jnp.tile(sin, (1, 2))


def _rope_kernel(q_ref, k_ref, v_ref, cos_ref, sin_ref, qo_ref, ko_ref, vto_ref):
    cos = cos_ref[...]
    sin = sin_ref[...]
    nf = DA_HD // 4
    lane = lax.broadcasted_iota(jnp.int32, (ROW_TILE, LANES), 1)
    first_half = (lane & (2 * nf - 1)) < nf

    def rot(x):
        partner = jnp.where(first_half, pltpu.roll(x, LANES - nf, 1), pltpu.roll(x, nf, 1))
        return x * cos + partner * sin

    for h in range(DA_HEADS):
        cols = slice(h * DA_VD, (h + 1) * DA_VD)
        qo_ref[:, cols] = (rot(q_ref[:, cols]) * (DA_HD ** -0.5)).astype(BF16)
        ko_ref[:, cols] = rot(k_ref[:, cols]).astype(BF16)
        vto_ref[cols, :] = v_ref[:, cols].T.astype(BF16)


def _rope(lay, proj, cos, sin):
    t = proj.shape[0]
    nl = lay.nl
    main = lambda c: pl.BlockSpec((ROW_TILE, W_GROUP), lambda i: (i, c))
    tab = pl.BlockSpec((ROW_TILE, LANES), lambda i: (jnp.where(i < lay.b, 0, 1 + (i - lay.b) % nl), 0))

    def key_tile(i):
        j = jnp.maximum(i - lay.b, 0)
        return jnp.where(i < lay.b, i * (nl + 1), (j // nl) * (nl + 1) + 1 + j % nl)

    return pl.pallas_call(
        _rope_kernel,
        grid=(t // ROW_TILE,),
        in_specs=[main(6), main(7), main(8), tab, tab],
        out_specs=[pl.BlockSpec((ROW_TILE, W_GROUP), lambda i: (i, 0)),
                   pl.BlockSpec((ROW_TILE, W_GROUP), lambda i: (key_tile(i), 0)),
                   pl.BlockSpec((W_GROUP, ROW_TILE), lambda i: (0, key_tile(i)))],
        out_shape=[jax.ShapeDtypeStruct((t, W_GROUP), BF16),
                   jax.ShapeDtypeStruct((t, W_GROUP), BF16),
                   jax.ShapeDtypeStruct((W_GROUP, t), BF16)],
        compiler_params=_cparams(1),
        name="rope",
    )(proj, proj, proj, cos, sin)


def _kv_chunk(n_keys):
    return max(c for c in range(ROW_TILE, MAX_KV_CHUNK + 1, ROW_TILE) if n_keys % c == 0)


def _attn_kernel(lam_init, n_ctx, chunk, q_ref, k_ref, vt_ref, lp_ref, g_ref, o_ref, qqt, m_scr, acc):
    qi = pl.program_id(2)
    tq = q_ref.shape[0]
    n_keys = k_ref.shape[0]

    qt = q_ref[...].astype(F32).T
    feat = lax.broadcasted_iota(jnp.int32, qt.shape, 0)
    qqt[:, 0:tq] = jnp.where(feat < DA_HD, qt, 0.0).astype(BF16)
    qqt[:, tq:] = jnp.where(feat >= DA_HD, qt, 0.0).astype(BF16)
    m_scr[...] = jnp.full_like(m_scr, -jnp.inf)
    acc[...] = jnp.zeros_like(acc)

    def scores(start, size):
        s = jnp.dot(k_ref[start:start + size, :], qqt[...], preferred_element_type=F32)
        return s, jnp.max(s, axis=0, keepdims=True)

    def accumulate(start, size, s, s_max):
        m_prev = m_scr[0:1, :]
        m_new = jnp.maximum(m_prev, s_max)
        alpha = jnp.exp(m_prev - m_new)
        p = jnp.exp(s - m_new).astype(BF16)
        vt = jnp.concatenate([vt_ref[:, start:start + size], jnp.ones((ONES_ROWS, size), BF16)], axis=0)
        acc[...] = alpha * acc[...] + jnp.dot(vt, p, preferred_element_type=F32)
        m_scr[...] = jnp.broadcast_to(m_new, m_scr.shape)

    @pl.when(qi == 0)
    def _():
        accumulate(0, n_ctx, *scores(0, n_ctx))

    @pl.when(qi > 0)
    def _():
        n_chunks = n_keys // chunk
        cur = scores(0, chunk)
        for c in range(n_chunks):
            nxt = scores((c + 1) * chunk, chunk) if c + 1 < n_chunks else None
            accumulate(c * chunk, chunk, *cur)
            cur = nxt

    lp = lp_ref[...]
    lam = (jnp.exp(jnp.sum(lp[0:1] * lp[1:2], axis=-1, keepdims=True))
           - jnp.exp(jnp.sum(lp[2:3] * lp[3:4], axis=-1, keepdims=True)) + lam_init)
    den = acc[DA_VD:DA_VD + 1, :]
    ot = acc[0:DA_VD, 0:tq] / den[:, 0:tq] - lam * (acc[0:DA_VD, tq:] / den[:, tq:])
    o = ot.T
    r = lax.rsqrt(jnp.mean(o * o, axis=-1, keepdims=True) + EPS)
    o_ref[...] = ((o * r) * g_ref[...] * (1.0 - lam_init)).astype(o_ref.dtype)


def _attention(lay, qs, kr, vt, da_lam, da_norm, lam_init):
    t = qs.shape[0]
    nl = lay.nl
    tq = ROW_TILE
    n_keys = lay.n_ctx + lay.seq
    q_spec = pl.BlockSpec((tq, DA_VD), lambda b, h, qi: (jnp.where(qi == 0, b, lay.b + b * nl + qi - 1), h))
    const = lambda shape: pl.BlockSpec(shape, lambda b, h, qi: (0,) * len(shape))
    return pl.pallas_call(
        functools.partial(_attn_kernel, lam_init, lay.n_ctx, _kv_chunk(n_keys)),
        grid=(lay.b, DA_HEADS, 1 + nl),
        in_specs=[q_spec,
                  pl.BlockSpec((n_keys, DA_VD), lambda b, h, qi: (b, h)),
                  pl.BlockSpec((DA_VD, n_keys), lambda b, h, qi: (h, b)),
                  const((4, DA_HD)), const((1, DA_VD))],
        out_specs=q_spec,
        out_shape=jax.ShapeDtypeStruct((t, W_GROUP), BF16),
        scratch_shapes=[pltpu.VMEM((DA_VD, 2 * tq), BF16),
                        pltpu.VMEM((SUBLANES, 2 * tq), F32),
                        pltpu.VMEM((DA_VD + ONES_ROWS, 2 * tq), F32)],
        compiler_params=_cparams(3),
        name="attention",
    )(qs, kr, vt, da_lam, da_norm.reshape(1, DA_VD))


def _outproj_kernel(ya_ref, yb_ref, yc_ref, yd_ref, w_ref, x_ref, m_ref, o_ref):
    y = jnp.concatenate([ya_ref[...], yb_ref[...], yc_ref[...], yd_ref[...]], axis=-1)
    o_ref[...] = x_ref[...] + m_ref[2:3, :] * jnp.dot(y, w_ref[...], preferred_element_type=F32)


def _out_projection(lay, ys, w_bf, x_all, mods):
    t, d = x_all.shape
    tn = 1024
    y_spec = pl.BlockSpec((MM_TILE, W_GROUP), lambda j, i: (i, 0))
    return pl.pallas_call(
        _outproj_kernel,
        grid=(d // tn, t // MM_TILE),
        in_specs=[y_spec] * 4 + [
            pl.BlockSpec((d, tn), lambda j, i: (0, j)),
            pl.BlockSpec((MM_TILE, tn), lambda j, i: (i, j)),
            pl.BlockSpec((None, N_MOD, tn), lambda j, i: (lay.mod_row_of_mm_tile(i), 0, j))],
        out_specs=pl.BlockSpec((MM_TILE, tn), lambda j, i: (i, j)),
        out_shape=jax.ShapeDtypeStruct((t, d), F32),
        compiler_params=_cparams(2),
        name="out_projection",
    )(*ys, w_bf, x_all, mods)


R_E1, R_E2, R_W1, R_W2, R_RANK1, R_RANK2 = range(6)


def _split_store(ref, value):
    rows = value.shape[0]
    for j in range(value.shape[1] // LANES):
        ref[pl.ds(j, rows, stride=ROW_CHUNKS), :] = value[:, j * LANES:(j + 1) * LANES]


def _router_kernel(x_ref, g_ref, m_ref, whi_ref, wlo_ref, b_ref, fl_ref, rec_ref, cnt_ref, counts):
    i = pl.program_id(0)

    @pl.when(i == 0)
    def _():
        counts[...] = jnp.zeros_like(counts)

    fl = _rms_mod(x_ref[...], g_ref[...], m_ref[3:4, :], m_ref[4:5, :])
    _split_store(fl_ref, fl)
    fl_hi = fl.astype(BF16)
    fl_lo = (fl - fl_hi.astype(F32)).astype(BF16)
    dot = functools.partial(jnp.dot, preferred_element_type=F32)
    logit = (dot(fl_hi, whi_ref[...]) + (dot(fl_hi, wlo_ref[...]) + dot(fl_lo, whi_ref[...]))) + b_ref[...]
    tm = logit.shape[0]
    lane = lax.broadcasted_iota(jnp.int32, (tm, LANES), 1).astype(F32)
    neg = -jnp.inf

    def first_argmax(vals):
        top = jnp.max(vals, axis=-1, keepdims=True)
        return top, jnp.min(jnp.where(vals == top, lane, float(LANES)), axis=-1, keepdims=True)

    is_grp = (lane >= N_EXPERTS) & (lane < N_EXPERTS + N_GROUPS)
    g_top, g_lane = first_argmax(jnp.where(is_grp, logit, neg))
    g_w = 1.0 / jnp.sum(jnp.where(is_grp, jnp.exp(logit - g_top), 0.0), axis=-1, keepdims=True)

    grp_start = (g_lane - N_EXPERTS) * EXP_PER_GROUP
    in_grp = (lane >= grp_start) & (lane < grp_start + EXP_PER_GROUP)
    v1, e1 = first_argmax(jnp.where(in_grp, logit, neg))
    v2, e2 = first_argmax(jnp.where(in_grp & (lane != e1), logit, neg))
    z = jnp.exp(v2 - v1)
    w1 = g_w / (1.0 + z)
    w2 = g_w * z / (1.0 + z)

    oh1 = lane == e1
    oh2 = lane == e2
    oh = jnp.where(oh1 | oh2, 1.0, 0.0).astype(BF16)
    r_i = lax.broadcasted_iota(jnp.int32, (tm, tm), 0)
    c_i = lax.broadcasted_iota(jnp.int32, (tm, tm), 1)
    below = jnp.where(c_i < r_i, 1.0, 0.0).astype(BF16)
    before = counts[0:1, :] + jnp.dot(below, oh, preferred_element_type=F32)
    rank1 = jnp.sum(jnp.where(oh1, before, 0.0), axis=-1, keepdims=True)
    rank2 = jnp.sum(jnp.where(oh2, before, 0.0), axis=-1, keepdims=True)
    counts[...] = counts[...] + jnp.sum(oh.astype(F32), axis=0, keepdims=True)
    cnt_ref[...] = counts[...]

    rec = jnp.zeros((tm, LANES), F32)
    for slot, val in ((R_E1, e1), (R_E2, e2), (R_W1, w1), (R_W2, w2), (R_RANK1, rank1), (R_RANK2, rank2)):
        rec = jnp.where(lane == slot, val, rec)
    rec_ref[...] = rec


def _router(lay, x_all, gain, mods, w_exp, b_exp, w_grp, b_grp, first_tile):
    t, d = x_all.shape
    assert d == ROW_CHUNKS * LANES
    n_rows = t - first_tile * MM_TILE
    pad = LANES - N_EXPERTS - N_GROUPS
    w = jnp.concatenate([w_exp, w_grp, jnp.zeros((d, pad), F32)], axis=1)
    b = jnp.concatenate([b_exp, b_grp, jnp.zeros((pad,), F32)]).reshape(1, LANES)
    w_hi = w.astype(BF16)
    w_lo = (w - w_hi.astype(F32)).astype(BF16)
    return pl.pallas_call(
        _router_kernel,
        grid=(n_rows // MM_TILE,),
        in_specs=[pl.BlockSpec((MM_TILE, d), lambda i: (i + first_tile, 0)),
                  pl.BlockSpec((1, d), lambda i: (0, 0)),
                  pl.BlockSpec((None, N_MOD, d), lambda i: (lay.mod_row_of_mm_tile(i + first_tile), 0, 0)),
                  pl.BlockSpec((d, LANES), lambda i: (0, 0)),
                  pl.BlockSpec((d, LANES), lambda i: (0, 0)),
                  pl.BlockSpec((1, LANES), lambda i: (0, 0))],
        out_specs=[pl.BlockSpec((MM_TILE * ROW_CHUNKS, LANES), lambda i: (i, 0)),
                   pl.BlockSpec((MM_TILE, LANES), lambda i: (i, 0)),
                   pl.BlockSpec((SUBLANES, LANES), lambda i: (0, 0))],
        out_shape=[jax.ShapeDtypeStruct((n_rows * ROW_CHUNKS, LANES), F32),
                   jax.ShapeDtypeStruct((n_rows, LANES), F32),
                   jax.ShapeDtypeStruct((SUBLANES, LANES), F32)],
        scratch_shapes=[pltpu.VMEM((SUBLANES, LANES), F32)],
        compiler_params=_cparams(1),
        name="moe_router",
    )(x_all, gain.reshape(1, d), mods, w_hi, w_lo, b)


def _token_copy(src_hbm, src_token, buf, dst_row, sem):
    src = pl.multiple_of(src_token * ROW_CHUNKS, ROW_CHUNKS)
    dst = dst_row * ROW_PITCH
    if not isinstance(dst, int):
        dst = pl.multiple_of(dst, SUBLANES)
    return pltpu.make_async_copy(src_hbm.at[pl.ds(src, ROW_CHUNKS), :], buf.at[pl.ds(dst, ROW_CHUNKS), :], sem)


def _row_copy(src_hbm, src_row, buf, dst_row, sem):
    return pltpu.make_async_copy(src_hbm.at[pl.ds(src_row, 1), :], buf.at[pl.ds(dst_row, 1), :], sem)


def _gathered(buf, rows, first_chunk, n_chunks):
    parts = [buf[pl.ds(j, rows, stride=ROW_PITCH), :] for j in range(first_chunk, first_chunk + n_chunks)]
    return parts[0] if n_chunks == 1 else jnp.concatenate(parts, axis=1)


def _expert_kernel(te_ref, src_ref, nt_ref, fl_hbm, wup_ref, wdn_ref, o_ref,
                   xbuf, sems, wup_bf, wdn_bf):
    i = pl.program_id(0)
    nt = nt_ref[0]
    slot = i % 2

    def start_gather(tile, slot_):
        def body(r, c):
            _token_copy(fl_hbm, src_ref[tile * EXP_TILE + r], xbuf.at[slot_], r, sems.at[slot_]).start()
            return c
        lax.fori_loop(0, EXP_TILE, body, 0, unroll=DMA_UNROLL)

    def wait_gather(slot_):
        for r in range(EXP_TILE):
            _token_copy(fl_hbm, 0, xbuf.at[slot_], r, sems.at[slot_]).wait()

    @pl.when(i == 0)
    def _():
        start_gather(0, 0)

    @pl.when(i < nt)
    def _():
        expert_changed = (i == 0) | (te_ref[i] != te_ref[jnp.maximum(i - 1, 0)])

        @pl.when(expert_changed)
        def _():
            wup_bf[...] = wup_ref[0].astype(BF16)
            wdn_bf[...] = wdn_ref[0].astype(BF16)

        wait_gather(slot)
        x = _gathered(xbuf.at[slot], EXP_TILE, 0, ROW_CHUNKS).astype(BF16)
        hu = jnp.dot(x, wup_bf[...], preferred_element_type=F32)
        h = (jax.nn.silu(hu[:, :D_EXPERT]) * hu[:, D_EXPERT:]).astype(BF16)
        part = ROW_CHUNKS // ISSUE_PARTS
        rows = EXP_TILE // ISSUE_PARTS
        for c in range(ISSUE_PARTS):
            cols = slice(c * part * LANES, (c + 1) * part * LANES)
            o_ref[:, cols] = jnp.dot(h, wdn_bf[:, cols], preferred_element_type=F32)
            for r in range(c * rows, (c + 1) * rows):
                _token_copy(fl_hbm, src_ref[(i + 1) * EXP_TILE + r], xbuf.at[1 - slot], r,
                          sems.at[1 - slot]).start()

    @pl.when(i == nt)
    def _():
        wait_gather(slot)

    @pl.when(i >= nt)
    def _():
        o_ref[...] = jnp.zeros_like(o_ref)


def _experts(fl, tile_expert, src_rows, n_tiles_used, w_up, w_down, n_tiles):
    d = w_up.shape[1]
    last = lambda i, nt: jnp.minimum(i, nt[0] - 1)
    grid_spec = pltpu.PrefetchScalarGridSpec(
        num_scalar_prefetch=3,
        grid=(n_tiles,),
        in_specs=[pl.BlockSpec(memory_space=pl.ANY),
                  pl.BlockSpec((1, d, 2 * D_EXPERT), lambda i, te, src, nt: (te[last(i, nt)], 0, 0)),
                  pl.BlockSpec((1, D_EXPERT, d), lambda i, te, src, nt: (te[last(i, nt)], 0, 0))],
        out_specs=pl.BlockSpec((EXP_TILE, d), lambda i, te, src, nt: (i, 0)),
        scratch_shapes=[pltpu.VMEM((2, EXP_TILE * ROW_PITCH, LANES), F32),
                        pltpu.SemaphoreType.DMA((2,)),
                        pltpu.VMEM((d, 2 * D_EXPERT), BF16),
                        pltpu.VMEM((D_EXPERT, d), BF16)])
    return pl.pallas_call(
        _expert_kernel,
        grid_spec=grid_spec,
        out_shape=jax.ShapeDtypeStruct((n_tiles * EXP_TILE, d), F32),
        compiler_params=_cparams(1),
        name="moe_experts",
    )(tile_expert, src_rows, n_tiles_used, fl, w_up, w_down)


def _combine_kernel(final, pos_ref, ys_hbm, x_ref, rec_ref, m_ref, gf_ref, o_ref, ybuf, sems):
    i = pl.program_id(0)
    n = pl.num_programs(0)
    slot = i % 2

    def start_gather(tile, slot_):
        def body(r, c):
            for k in range(2):
                _row_copy(ys_hbm, pos_ref[(tile * ROW_TILE + r) * 2 + k], ybuf.at[slot_, k], r,
                          sems.at[slot_]).start()
            return c
        lax.fori_loop(0, ROW_TILE, body, 0, unroll=DMA_UNROLL)

    def wait_gather(slot_):
        for r in range(2 * ROW_TILE):
            _row_copy(ys_hbm, 0, ybuf.at[slot_, r % 2], r // 2, sems.at[slot_]).wait()

    @pl.when(i == 0)
    def _():
        start_gather(0, 0)

    wait_gather(slot)
    rec = rec_ref[...]
    w1 = rec[:, R_W1:R_W1 + 1]
    w2 = rec[:, R_W2:R_W2 + 1]
    part = o_ref.shape[1] // ISSUE_PARTS
    toks = ROW_TILE // ISSUE_PARTS
    for c in range(ISSUE_PARTS):
        cols = slice(c * part, (c + 1) * part)
        y = w1 * ybuf[slot, 0, :, cols] + w2 * ybuf[slot, 1, :, cols]
        o_ref[:, cols] = x_ref[:, cols] + m_ref[5:6, cols] * y
        for r in range(c * toks, (c + 1) * toks):
            for k in range(2):
                _row_copy(ys_hbm, pos_ref[((i + 1) * ROW_TILE + r) * 2 + k], ybuf.at[1 - slot, k], r,
                          sems.at[1 - slot]).start()
    if final:
        x = o_ref[...]
        r = lax.rsqrt(jnp.mean(x * x, axis=-1, keepdims=True) + EPS)
        o_ref[...] = (x * r) * gf_ref[...]

    @pl.when(i == n - 1)
    def _():
        wait_gather(1 - slot)


def _combine(lay, pos, ys, x_all, rec, mods, gain_final, first_tile, final):
    d = x_all.shape[1]
    n_rows = rec.shape[0]
    off = first_tile * (MM_TILE // ROW_TILE)
    grid_spec = pltpu.PrefetchScalarGridSpec(
        num_scalar_prefetch=1,
        grid=(n_rows // ROW_TILE,),
        in_specs=[pl.BlockSpec(memory_space=pl.ANY),
                  pl.BlockSpec((ROW_TILE, d), lambda i, pos: (i + off, 0)),
                  pl.BlockSpec((ROW_TILE, LANES), lambda i, pos: (i, 0)),
                  pl.BlockSpec((None, N_MOD, d), lambda i, pos: (lay.mod_row_of_seq_tile(i + off), 0, 0)),
                  pl.BlockSpec((1, d), lambda i, pos: (0, 0))],
        out_specs=pl.BlockSpec((ROW_TILE, d), lambda i, pos: (i, 0)),
        scratch_shapes=[pltpu.VMEM((2, 2, ROW_TILE, d), F32),
                        pltpu.SemaphoreType.DMA((2,))])
    return pl.pallas_call(
        functools.partial(_combine_kernel, final),
        grid_spec=grid_spec,
        out_shape=jax.ShapeDtypeStruct((n_rows, d), F32),
        compiler_params=_cparams(1),
        name="moe_combine",
    )(jnp.concatenate([pos, jnp.zeros((2 * ROW_TILE,), jnp.int32)]), ys, x_all, rec, mods,
      gain_final.reshape(1, d))


def _dispatch_plan(rec, counts, n_tiles):
    n_rows = rec.shape[0]
    cnt = counts[0, :N_EXPERTS].astype(jnp.int32)
    tiles_per = (cnt + EXP_TILE - 1) // EXP_TILE
    tile_end = jnp.cumsum(tiles_per)
    offset = (tile_end - tiles_per) * EXP_TILE
    e_sel = rec[:, R_E1:R_E2 + 1].astype(jnp.int32)
    rank = rec[:, R_RANK1:R_RANK2 + 1].astype(jnp.int32)
    pos = (offset[e_sel] + rank).reshape(-1)
    token = jnp.repeat(jnp.arange(n_rows, dtype=jnp.int32), 2)
    src_rows = jnp.zeros((n_tiles * EXP_TILE,), jnp.int32).at[pos].set(token, unique_indices=True)
    n_used = tile_end[-1]
    tile_ids = jnp.minimum(jnp.arange(n_tiles, dtype=jnp.int32), n_used - 1)
    tile_expert = jnp.sum(tile_end[None, :] <= tile_ids[:, None], axis=1).astype(jnp.int32)
    return pos, src_rows, tile_expert, n_used.reshape(1).astype(jnp.int32)


def _moe(lay, x_all, gain, mods, w_grp, b_grp, w_exp, b_exp, w_up_all, w_down_all, layer, gain_final,
         first_tile, final):
    fl, rec, counts = _router(lay, x_all, gain, mods, w_exp, b_exp, w_grp, b_grp, first_tile)
    n_rows = rec.shape[0]
    assert (2 * n_rows) % EXP_TILE == 0
    n_tiles = (2 * n_rows) // EXP_TILE + N_EXPERTS
    pos, src_rows, tile_expert, n_used = _dispatch_plan(rec, counts, n_tiles)
    d = x_all.shape[1]
    w_up = w_up_all.reshape(-1, d, 2 * D_EXPERT)
    w_down = w_down_all.reshape(-1, D_EXPERT, d)
    ys = _experts(fl, tile_expert + layer * N_EXPERTS, src_rows, n_used, w_up, w_down, n_tiles)
    return _combine(lay, pos, ys, x_all, rec, mods, gain_final, first_tile, final)


def kernel(x, c, ctx, c_ctx, mod_w, mod_b, norm_mix, norm_ffn, w_in, w_out, rg_conv_w, rg_conv_b,
           rg_gate_w, rg_gate_b, rg_lam, sc_conv_w, pool_w, pool_b, pool_scale, da_lam, da_norm,
           moe_grp_w, moe_grp_b, moe_exp_w, moe_exp_b, moe_up, moe_down, norm_final):
    n_batch, seq, d = x.shape
    n_ctx = ctx.shape[1]
    depth = mod_w.shape[0]
    lay = _Layout(n_batch, n_ctx, seq)
    assert 1 + n_batch <= SUBLANES

    cc = jnp.zeros((SUBLANES, d), F32).at[0].set(c_ctx).at[1:1 + n_batch].set(c)
    mods_all = _modulation(cc, mod_w, mod_b)
    cos, sin = _rope_tables(lay)
    x_all = jnp.concatenate([ctx.reshape(n_batch * n_ctx, d), x.reshape(n_batch * seq, d)], axis=0)

    for l in range(depth):
        last = l == depth - 1
        lam_init = 0.8 - 0.6 * math.exp(-0.3 * l)
        mods = mods_all[l]

        proj = _in_projection(lay, x_all, norm_mix[l], mods, w_in[l].astype(BF16))
        hf = _rglru(lay, proj, rg_conv_w[l], rg_conv_b[l], rg_gate_w[l, 0], rg_gate_b[l, 0],
                    rg_lam[l, 0], reverse=False)
        ya = _rglru(lay, proj, rg_conv_w[l], rg_conv_b[l], rg_gate_w[l, 1], rg_gate_b[l, 1],
                    rg_lam[l, 1], reverse=True, hf=hf)
        yb, yc = _conv_pool(lay, proj, sc_conv_w[l], pool_w[l], pool_b[l], pool_scale[l])
        qs, kr, vt = _rope(lay, proj, cos, sin)
        yd = _attention(lay, qs, kr, vt, da_lam[l], da_norm[l], lam_init)
        x_all = _out_projection(lay, (ya, yb, yc, yd), w_out[l].astype(BF16), x_all, mods)

        first_tile = 1 if last else 0
        out = _moe(lay, x_all, norm_ffn[l], mods, moe_grp_w[l], moe_grp_b[l], moe_exp_w[l], moe_exp_b[l],
                   moe_up, moe_down, l, norm_final, first_tile, last)
        if last:
            return out.reshape(n_batch, seq, d)
        x_all = out
```

```python
import functools
import math

import jax
import jax.numpy as jnp
from jax import lax
from jax.experimental import pallas as pl
from jax.experimental.pallas import tpu as pltpu

F32 = jnp.float32
BF16 = jnp.bfloat16

W_GROUP = 512
N_PROJ = 9
RG_BLOCKS = 4
RG_BW = 128
RG_C = 8.0
POOL_WINDOWS = (2, 4, 8, 16)
POOL_GW = 128
DA_HEADS = 4
DA_HD = 64
DA_VD = 128
ROPE_THETA = 10000.0
GRID_W = 64
N_GROUPS = 4
EXP_PER_GROUP = 8
N_EXPERTS = 32
D_EXPERT = 512
N_MOD = 6
EPS = 1e-6

SUBLANES = 8
LANES = 128
ROW_TILE = 256
MM_TILE = 512
MAX_KV_CHUNK = 2816
ONES_ROWS = 16
EXP_TILE = 256
DMA_UNROLL = 8
ISSUE_PARTS = 8
ROW_CHUNKS = 16
ROW_PITCH = 24
VMEM_LIMIT = 56 * 1024 * 1024


def _cparams(n_axes):
    return pltpu.CompilerParams(dimension_semantics=("arbitrary",) * n_axes,
                                vmem_limit_bytes=VMEM_LIMIT)


def _mod_kernel(c_ref, w_ref, b_ref, o_ref):
    c = c_ref[...]
    s = c * jax.nn.sigmoid(c)
    o_ref[0] = jnp.dot(s.astype(BF16), w_ref[0].astype(BF16), preferred_element_type=F32) + b_ref[0]


def _modulation(cc, mod_w, mod_b):
    depth, d, n = mod_w.shape
    tn = 1024
    out = pl.pallas_call(
        _mod_kernel,
        grid=(depth, n // tn),
        in_specs=[pl.BlockSpec((SUBLANES, d), lambda l, j: (0, 0)),
                  pl.BlockSpec((1, d, tn), lambda l, j: (l, 0, j)),
                  pl.BlockSpec((1, 1, tn), lambda l, j: (l, 0, j))],
        out_specs=pl.BlockSpec((1, SUBLANES, tn), lambda l, j: (l, 0, j)),
        out_shape=jax.ShapeDtypeStruct((depth, SUBLANES, n), F32),
        compiler_params=_cparams(2),
        name="modulation",
    )(cc, mod_w, mod_b.reshape(depth, 1, n))
    return out.reshape(depth, SUBLANES, N_MOD, d)


def _rms_mod(x, g, shift, scale):
    r = lax.rsqrt(jnp.mean(x * x, axis=-1, keepdims=True) + EPS)
    return (x * r) * g * (1.0 + scale) + shift


class _Layout:
    def __init__(self, n_batch, n_ctx, seq):
        assert n_ctx == ROW_TILE, "one sequence tile per context sequence"
        assert n_batch * n_ctx == MM_TILE, "context rows fill exactly one projection tile"
        assert seq % MM_TILE == 0 and seq % GRID_W == 0
        self.b = n_batch
        self.n_ctx = n_ctx
        self.seq = seq
        self.rows = n_batch * (n_ctx + seq)
        self.nl = seq // ROW_TILE
        self.n_tiles = self.rows // ROW_TILE

    def mod_row_of_seq_tile(self, i):
        return jnp.where(i < self.b, 0, 1 + (i - self.b) // self.nl)

    def mod_row_of_mm_tile(self, i):
        return jnp.where(i < 1, 0, 1 + (i - 1) // (self.seq // MM_TILE))


def _row_tile_specs(x_src, tile_of):
    if not isinstance(x_src, tuple):
        return [pl.BlockSpec((MM_TILE, x_src.shape[1]), lambda *ids: (tile_of(*ids), 0))]
    ctx_rows, lat_rows = x_src
    assert ctx_rows.shape[0] == MM_TILE
    d = ctx_rows.shape[1]
    return [pl.BlockSpec((MM_TILE, d), lambda *ids: (0, 0)),
            pl.BlockSpec((MM_TILE, d), lambda *ids: (jnp.maximum(tile_of(*ids) - 1, 0), 0))]


def _row_tile(refs, tile):
    if len(refs) == 1:
        return refs[0][...]
    return jnp.where(tile == 0, refs[0][...], refs[1][...])


def _src_list(x_src):
    return list(x_src) if isinstance(x_src, tuple) else [x_src]


def _inproj_kernel(n_src, *refs):
    x_refs, (g_ref, m_ref, w_ref, o_ref) = refs[:n_src], refs[n_src:]
    h = _rms_mod(_row_tile(x_refs, pl.program_id(1)), g_ref[...], m_ref[0:1, :], m_ref[1:2, :])
    o_ref[...] = jnp.dot(h.astype(BF16), w_ref[...], preferred_element_type=F32)


def _in_projection(lay, x_src, gain, mods, w_bf):
    xs = _src_list(x_src)
    t, d = lay.rows, xs[0].shape[1]
    n = w_bf.shape[1]
    tn = n // 2
    return pl.pallas_call(
        functools.partial(_inproj_kernel, len(xs)),
        grid=(n // tn, t // MM_TILE),
        in_specs=_row_tile_specs(x_src, lambda j, i: i) + [
                  pl.BlockSpec((1, d), lambda j, i: (0, 0)),
                  pl.BlockSpec((None, N_MOD, d), lambda j, i: (lay.mod_row_of_mm_tile(i), 0, 0)),
                  pl.BlockSpec((d, tn), lambda j, i: (0, j))],
        out_specs=pl.BlockSpec((MM_TILE, tn), lambda j, i: (i, j)),
        out_shape=jax.ShapeDtypeStruct((t, n), F32),
        compiler_params=_cparams(2),
        name="in_projection",
    )(*xs, gain.reshape(1, d), mods, w_bf)


def _fill_ext(ext, prev_ref, x, next_ref, is_first, is_last):
    ext[0:SUBLANES, :] = jnp.where(is_first, 0.0, prev_ref[...])
    ext[SUBLANES:SUBLANES + ROW_TILE, :] = x
    ext[SUBLANES + ROW_TILE:, :] = jnp.where(is_last, 0.0, next_ref[...])


def _shifted(ext, off, cols=slice(None)):
    return ext[SUBLANES + off:SUBLANES + off + ROW_TILE, cols]


def _rglru_kernel(reverse, n_lat_tiles, *refs):
    if reverse:
        (x_ref, xp_ref, xn_ref, cw_ref, cb_ref, gw_ref, gb_ref, lam_ref, hf_ref, ag_ref,
         o_ref, ext, a_scr, b_scr, carry) = refs
    else:
        (x_ref, xp_ref, xn_ref, cw_ref, cb_ref, gw_ref, gb_ref, lam_ref,
         o_ref, ext, a_scr, b_scr, carry) = refs
    s = pl.program_id(1)
    if reverse:
        is_first = (s == 0) | (s == n_lat_tiles)
        is_last = (s == 0) | (s == 1)
    else:
        is_first = (s == 0) | (s == 1)
        is_last = (s == 0) | (s == n_lat_tiles)

    @pl.when(s == 0)
    def _():
        carry[...] = jnp.zeros_like(carry)

    _fill_ext(ext, xp_ref, x_ref[...], xn_ref, is_first, is_last)
    u = cb_ref[...] + sum(cw_ref[k:k + 1, :] * _shifted(ext, k - 2) for k in range(4))

    sp = -lam_ref[...]
    sp = jnp.maximum(sp, 0.0) + jnp.log1p(jnp.exp(-jnp.abs(sp)))
    for n in range(RG_BLOCKS):
        cols = slice(n * RG_BW, (n + 1) * RG_BW)
        un = u[:, cols]
        g = jnp.dot(un.astype(BF16), gw_ref[n], preferred_element_type=F32)
        r = jax.nn.sigmoid(g[:, :RG_BW] + gb_ref[0:1, cols])
        gi = jax.nn.sigmoid(g[:, RG_BW:] + gb_ref[1:2, cols])
        log_a = -RG_C * r * sp[:, cols]
        a_scr[:, cols] = jnp.exp(log_a)
        th = jnp.tanh(log_a)
        b_scr[:, cols] = jnp.sqrt(-2.0 * th / (1.0 - th)) * (gi * un)

    row = lax.broadcasted_iota(jnp.int32, (SUBLANES, W_GROUP), 0)
    n_groups = ROW_TILE // SUBLANES

    def group(gidx, h_prev):
        g0 = (n_groups - 1 - gidx) if reverse else gidx
        start = pl.multiple_of(g0 * SUBLANES, SUBLANES)
        a8 = a_scr[pl.ds(start, SUBLANES), :]
        b8 = b_scr[pl.ds(start, SUBLANES), :]
        for sh in (1, 2, 4):
            rot = (SUBLANES - sh) if reverse else sh
            a_sh = pltpu.roll(a8, rot, 0)
            b_sh = pltpu.roll(b8, rot, 0)
            valid = (row < SUBLANES - sh) if reverse else (row >= sh)
            b8 = jnp.where(valid, a8 * b_sh + b8, b8)
            a8 = jnp.where(valid, a8 * a_sh, a8)
        h8 = a8 * h_prev + b8
        a_scr[pl.ds(start, SUBLANES), :] = h8
        edge = 0 if reverse else SUBLANES - 1
        return h8[edge:edge + 1, :]

    carry[...] = lax.fori_loop(0, n_groups, group, carry[...], unroll=4)

    h = a_scr[...]
    if reverse:
        o_ref[...] = (jax.nn.gelu(ag_ref[...]) * (hf_ref[...] + h)).astype(o_ref.dtype)
    else:
        o_ref[...] = h


def _rglru(lay, proj, conv_w, conv_b, gate_w, gate_b, lam, reverse, hf=None):
    t = proj.shape[0]
    nl = lay.nl
    nb8 = ROW_TILE // SUBLANES

    def tile(b, s):
        lat = (nl - s) if reverse else (s - 1)
        return jnp.where(s == 0, b, lay.b + b * nl + lat)

    main = lambda c: pl.BlockSpec((ROW_TILE, W_GROUP), lambda b, s: (tile(b, s), c))
    prev = pl.BlockSpec((SUBLANES, W_GROUP), lambda b, s: (jnp.maximum(tile(b, s) * nb8 - 1, 0), 0))
    nxt = pl.BlockSpec((SUBLANES, W_GROUP),
                       lambda b, s: (jnp.minimum((tile(b, s) + 1) * nb8, t // SUBLANES - 1), 0))
    const = lambda shape: pl.BlockSpec(shape, lambda b, s: (0,) * len(shape))
    gw = jnp.transpose(gate_w, (1, 2, 0, 3)).reshape(RG_BLOCKS, RG_BW, 2 * RG_BW).astype(BF16)
    gb = gate_b.reshape(2, W_GROUP)
    in_specs = [main(0), prev, nxt, const((4, W_GROUP)), const((1, W_GROUP)),
                const((RG_BLOCKS, RG_BW, 2 * RG_BW)), const((2, W_GROUP)), const((1, W_GROUP))]
    args = [proj, proj, proj, conv_w, conv_b.reshape(1, W_GROUP), gw, gb, lam.reshape(1, W_GROUP)]
    if reverse:
        in_specs += [pl.BlockSpec((ROW_TILE, W_GROUP), lambda b, s: (tile(b, s), 0)), main(1)]
        args += [hf, proj]
    return pl.pallas_call(
        functools.partial(_rglru_kernel, reverse, nl),
        grid=(lay.b, 1 + nl),
        in_specs=in_specs,
        out_specs=pl.BlockSpec((ROW_TILE, W_GROUP), lambda b, s: (tile(b, s), 0)),
        out_shape=jax.ShapeDtypeStruct((t, W_GROUP), BF16 if reverse else F32),
        scratch_shapes=[pltpu.VMEM((ROW_TILE + 2 * SUBLANES, W_GROUP), F32),
                        pltpu.VMEM((ROW_TILE, W_GROUP), F32),
                        pltpu.VMEM((ROW_TILE, W_GROUP), F32),
                        pltpu.VMEM((1, W_GROUP), F32)],
        compiler_params=_cparams(2),
        name="rglru_bwd" if reverse else "rglru_fwd",
    )(*args)


def _conv_pool_kernel(lay, bb_ref, bc_ref, bcp_ref, bcn_ref, bx_ref, bxp_ref, bxn_ref,
                      cx_ref, cxp_ref, cxn_ref, scw_ref, pw_ref, pb_ref, ps_ref,
                      yb_ref, yc_ref, ext):
    i = pl.program_id(0)
    j = jnp.maximum(i - lay.b, 0) % lay.nl
    is_ctx = i < lay.b
    is_first = is_ctx | (j == 0)
    is_last = is_ctx | (j == lay.nl - 1)

    ext[0:SUBLANES, :] = jnp.where(is_first, 0.0, bcp_ref[...] * bxp_ref[...])
    ext[SUBLANES:SUBLANES + ROW_TILE, :] = bc_ref[...] * bx_ref[...]
    ext[SUBLANES + ROW_TILE:, :] = jnp.where(is_last, 0.0, bcn_ref[...] * bxn_ref[...])
    conv = sum(scw_ref[k:k + 1, :] * _shifted(ext, k - 1) for k in range(3))
    yb_ref[...] = (bb_ref[...] * conv).astype(yb_ref.dtype)

    x = cx_ref[...]
    _fill_ext(ext, cxp_ref, x, cxn_ref, is_first, is_last)
    seq_len = jnp.where(is_ctx, lay.n_ctx, lay.seq)
    pos = jnp.where(is_ctx, 0, j * ROW_TILE) + lax.broadcasted_iota(jnp.int32, (ROW_TILE, 1), 0)
    for gi, win in enumerate(POOL_WINDOWS):
        cols = slice(gi * POOL_GW, (gi + 1) * POOL_GW)
        half = win // 2
        tot = sum(_shifted(ext, off, cols) for off in range(-half, half))
        cnt = jnp.minimum(pos + half, seq_len) - jnp.maximum(pos - half, 0)
        dlt = tot / cnt.astype(F32) - x[:, cols]
        y = jnp.dot(dlt.astype(BF16), pw_ref[gi].astype(BF16), preferred_element_type=F32)
        yc_ref[:, cols] = ((y + pb_ref[gi:gi + 1, :]) * ps_ref[:, cols]).astype(yc_ref.dtype)


def _conv_pool(lay, proj, sc_w, pool_w, pool_b, pool_scale):
    t = proj.shape[0]
    nb8 = ROW_TILE // SUBLANES
    main = lambda c: pl.BlockSpec((ROW_TILE, W_GROUP), lambda i: (i, c))
    prev = lambda c: pl.BlockSpec((SUBLANES, W_GROUP), lambda i: (jnp.maximum(i * nb8 - 1, 0), c))
    nxt = lambda c: pl.BlockSpec((SUBLANES, W_GROUP),
                                 lambda i: (jnp.minimum((i + 1) * nb8, t // SUBLANES - 1), c))
    const = lambda shape: pl.BlockSpec(shape, lambda i: (0,) * len(shape))
    n_pool = len(POOL_WINDOWS)
    out = jax.ShapeDtypeStruct((t, W_GROUP), BF16)
    return pl.pallas_call(
        functools.partial(_conv_pool_kernel, lay),
        grid=(t // ROW_TILE,),
        in_specs=[main(2), main(3), prev(3), nxt(3), main(4), prev(4), nxt(4),
                  main(5), prev(5), nxt(5), const((3, W_GROUP)),
                  const((n_pool, POOL_GW, POOL_GW)), const((n_pool, POOL_GW)), const((1, W_GROUP))],
        out_specs=[pl.BlockSpec((ROW_TILE, W_GROUP), lambda i: (i, 0))] * 2,
        out_shape=[out, out],
        scratch_shapes=[pltpu.VMEM((ROW_TILE + 2 * SUBLANES, W_GROUP), F32)],
        compiler_params=_cparams(1),
        name="conv_pool",
    )(*([proj] * 10), sc_w, pool_w, pool_b, pool_scale.reshape(1, W_GROUP))


def _rope_tables(lay):
    nf = DA_HD // 4
    inv = ROPE_THETA ** (-jnp.arange(nf, dtype=F32) / nf)
    pos = jnp.arange(lay.seq, dtype=jnp.int32)
    ang_r = (pos // GRID_W).astype(F32)[:, None] * inv
    ang_c = (pos % GRID_W).astype(F32)[:, None] * inv
    cos = jnp.concatenate([jnp.cos(ang_r)] * 2 + [jnp.cos(ang_c)] * 2, axis=-1)
    sin = jnp.concatenate([-jnp.sin(ang_r), jnp.sin(ang_r), -jnp.sin(ang_c), jnp.sin(ang_c)], axis=-1)
    cos = jnp.concatenate([jnp.ones((ROW_TILE, DA_HD), F32), cos], axis=0)
    sin = jnp.concatenate([jnp.zeros((ROW_TILE, DA_HD), F32), sin], axis=0)
    return jnp.tile(cos, (1, 2)), jnp.tile(sin, (1, 2))


def _rope_kernel(q_ref, k_ref, v_ref, cos_ref, sin_ref, qo_ref, ko_ref, vto_ref):
    cos = cos_ref[...]
    sin = sin_ref[...]
    nf = DA_HD // 4
    lane = lax.broadcasted_iota(jnp.int32, (ROW_TILE, LANES), 1)
    first_half = (lane & (2 * nf - 1)) < nf

    def rot(x):
        partner = jnp.where(first_half, pltpu.roll(x, LANES - nf, 1), pltpu.roll(x, nf, 1))
        return x * cos + partner * sin

    for h in range(DA_HEADS):
        cols = slice(h * DA_VD, (h + 1) * DA_VD)
        qo_ref[:, cols] = (rot(q_ref[:, cols]) * (DA_HD ** -0.5)).astype(BF16)
        ko_ref[:, cols] = rot(k_ref[:, cols]).astype(BF16)
        vto_ref[cols, :] = v_ref[:, cols].T.astype(BF16)


def _rope(lay, proj, cos, sin):
    t = proj.shape[0]
    nl = lay.nl
    main = lambda c: pl.BlockSpec((ROW_TILE, W_GROUP), lambda i: (i, c))
    tab = pl.BlockSpec((ROW_TILE, LANES), lambda i: (jnp.where(i < lay.b, 0, 1 + (i - lay.b) % nl), 0))

    def key_tile(i):
        j = jnp.maximum(i - lay.b, 0)
        return jnp.where(i < lay.b, i * (nl + 1), (j // nl) * (nl + 1) + 1 + j % nl)

    return pl.pallas_call(
        _rope_kernel,
        grid=(t // ROW_TILE,),
        in_specs=[main(6), main(7), main(8), tab, tab],
        out_specs=[pl.BlockSpec((ROW_TILE, W_GROUP), lambda i: (i, 0)),
                   pl.BlockSpec((ROW_TILE, W_GROUP), lambda i: (key_tile(i), 0)),
                   pl.BlockSpec((W_GROUP, ROW_TILE), lambda i: (0, key_tile(i)))],
        out_shape=[jax.ShapeDtypeStruct((t, W_GROUP), BF16),
                   jax.ShapeDtypeStruct((t, W_GROUP), BF16),
                   jax.ShapeDtypeStruct((W_GROUP, t), BF16)],
        compiler_params=_cparams(1),
        name="rope",
    )(proj, proj, proj, cos, sin)


def _kv_chunk(n_keys):
    return max(c for c in range(ROW_TILE, MAX_KV_CHUNK + 1, ROW_TILE) if n_keys % c == 0)


def _attn_kernel(lam_init, n_ctx, chunk, q_ref, k_ref, vt_ref, lp_ref, g_ref, o_ref, qqt, m_scr, acc):
    qi = pl.program_id(2)
    tq = q_ref.shape[0]
    n_keys = k_ref.shape[0]

    qt = q_ref[...].astype(F32).T
    feat = lax.broadcasted_iota(jnp.int32, qt.shape, 0)
    qqt[:, 0:tq] = jnp.where(feat < DA_HD, qt, 0.0).astype(BF16)
    qqt[:, tq:] = jnp.where(feat >= DA_HD, qt, 0.0).astype(BF16)
    m_scr[...] = jnp.full_like(m_scr, -jnp.inf)
    acc[...] = jnp.zeros_like(acc)

    def scores(start, size):
        s = jnp.dot(k_ref[start:start + size, :], qqt[...], preferred_element_type=F32)
        return s, jnp.max(s, axis=0, keepdims=True)

    def accumulate(start, size, s, s_max):
        m_prev = m_scr[0:1, :]
        m_new = jnp.maximum(m_prev, s_max)
        alpha = jnp.exp(m_prev - m_new)
        p = jnp.exp(s - m_new).astype(BF16)
        vt = jnp.concatenate([vt_ref[:, start:start + size], jnp.ones((ONES_ROWS, size), BF16)], axis=0)
        acc[...] = alpha * acc[...] + jnp.dot(vt, p, preferred_element_type=F32)
        m_scr[...] = jnp.broadcast_to(m_new, m_scr.shape)

    @pl.when(qi == 0)
    def _():
        accumulate(0, n_ctx, *scores(0, n_ctx))

    @pl.when(qi > 0)
    def _():
        n_chunks = n_keys // chunk
        cur = scores(0, chunk)
        for c in range(n_chunks):
            nxt = scores((c + 1) * chunk, chunk) if c + 1 < n_chunks else None
            accumulate(c * chunk, chunk, *cur)
            cur = nxt

    lp = lp_ref[...]
    lam = (jnp.exp(jnp.sum(lp[0:1] * lp[1:2], axis=-1, keepdims=True))
           - jnp.exp(jnp.sum(lp[2:3] * lp[3:4], axis=-1, keepdims=True)) + lam_init)
    den = acc[DA_VD:DA_VD + 1, :]
    ot = acc[0:DA_VD, 0:tq] / den[:, 0:tq] - lam * (acc[0:DA_VD, tq:] / den[:, tq:])
    o = ot.T
    r = lax.rsqrt(jnp.mean(o * o, axis=-1, keepdims=True) + EPS)
    o_ref[...] = ((o * r) * g_ref[...] * (1.0 - lam_init)).astype(o_ref.dtype)


def _attention(lay, qs, kr, vt, da_lam, da_norm, lam_init):
    t = qs.shape[0]
    nl = lay.nl
    tq = ROW_TILE
    n_keys = lay.n_ctx + lay.seq
    q_spec = pl.BlockSpec((tq, DA_VD), lambda b, h, qi: (jnp.where(qi == 0, b, lay.b + b * nl + qi - 1), h))
    const = lambda shape: pl.BlockSpec(shape, lambda b, h, qi: (0,) * len(shape))
    return pl.pallas_call(
        functools.partial(_attn_kernel, lam_init, lay.n_ctx, _kv_chunk(n_keys)),
        grid=(lay.b, DA_HEADS, 1 + nl),
        in_specs=[q_spec,
                  pl.BlockSpec((n_keys, DA_VD), lambda b, h, qi: (b, h)),
                  pl.BlockSpec((DA_VD, n_keys), lambda b, h, qi: (h, b)),
                  const((4, DA_HD)), const((1, DA_VD))],
        out_specs=q_spec,
        out_shape=jax.ShapeDtypeStruct((t, W_GROUP), BF16),
        scratch_shapes=[pltpu.VMEM((DA_VD, 2 * tq), BF16),
                        pltpu.VMEM((SUBLANES, 2 * tq), F32),
                        pltpu.VMEM((DA_VD + ONES_ROWS, 2 * tq), F32)],
        compiler_params=_cparams(3),
        name="attention",
    )(qs, kr, vt, da_lam, da_norm.reshape(1, DA_VD))


def _outproj_kernel(n_src, ya_ref, yb_ref, yc_ref, yd_ref, w_ref, m_ref, *refs):
    x_refs, o_ref = refs[:n_src], refs[n_src]
    y = jnp.concatenate([ya_ref[...], yb_ref[...], yc_ref[...], yd_ref[...]], axis=-1)
    x = _row_tile(x_refs, pl.program_id(0))
    o_ref[...] = x + m_ref[2:3, :] * jnp.dot(y, w_ref[...], preferred_element_type=F32)


def _out_projection(lay, ys, w_bf, x_src, mods):
    xs = _src_list(x_src)
    t, d = lay.rows, xs[0].shape[1]
    y_spec = pl.BlockSpec((MM_TILE, W_GROUP), lambda i: (i, 0))
    return pl.pallas_call(
        functools.partial(_outproj_kernel, len(xs)),
        grid=(t // MM_TILE,),
        in_specs=[y_spec] * 4 + [
            pl.BlockSpec((d, d), lambda i: (0, 0)),
            pl.BlockSpec((None, N_MOD, d), lambda i: (lay.mod_row_of_mm_tile(i), 0, 0))]
            + _row_tile_specs(x_src, lambda i: i),
        out_specs=pl.BlockSpec((MM_TILE, d), lambda i: (i, 0)),
        out_shape=jax.ShapeDtypeStruct((t, d), F32),
        compiler_params=_cparams(1),
        name="out_projection",
    )(*ys, w_bf, mods, *xs)


R_E1, R_E2, R_W1, R_W2, R_RANK1, R_RANK2 = range(6)


def _split_store(ref, value):
    rows = value.shape[0]
    for j in range(value.shape[1] // LANES):
        ref[pl.ds(j, rows, stride=ROW_CHUNKS), :] = value[:, j * LANES:(j + 1) * LANES]


def _router_kernel(x_ref, g_ref, m_ref, whi_ref, wlo_ref, b_ref, fl_ref, rec_ref, cnt_ref, counts):
    i = pl.program_id(0)

    @pl.when(i == 0)
    def _():
        counts[...] = jnp.zeros_like(counts)

    fl = _rms_mod(x_ref[...], g_ref[...], m_ref[3:4, :], m_ref[4:5, :])
    _split_store(fl_ref, fl)
    fl_hi = fl.astype(BF16)
    fl_lo = (fl - fl_hi.astype(F32)).astype(BF16)
    dot = functools.partial(jnp.dot, preferred_element_type=F32)
    logit = (dot(fl_hi, whi_ref[...]) + (dot(fl_hi, wlo_ref[...]) + dot(fl_lo, whi_ref[...]))) + b_ref[...]
    tm = logit.shape[0]
    lane = lax.broadcasted_iota(jnp.int32, (tm, LANES), 1).astype(F32)
    neg = -jnp.inf

    def first_argmax(vals):
        top = jnp.max(vals, axis=-1, keepdims=True)
        return top, jnp.min(jnp.where(vals == top, lane, float(LANES)), axis=-1, keepdims=True)

    is_grp = (lane >= N_EXPERTS) & (lane < N_EXPERTS + N_GROUPS)
    g_top, g_lane = first_argmax(jnp.where(is_grp, logit, neg))
    g_w = 1.0 / jnp.sum(jnp.where(is_grp, jnp.exp(logit - g_top), 0.0), axis=-1, keepdims=True)

    grp_start = (g_lane - N_EXPERTS) * EXP_PER_GROUP
    in_grp = (lane >= grp_start) & (lane < grp_start + EXP_PER_GROUP)
    v1, e1 = first_argmax(jnp.where(in_grp, logit, neg))
    v2, e2 = first_argmax(jnp.where(in_grp & (lane != e1), logit, neg))
    z = jnp.exp(v2 - v1)
    w1 = g_w / (1.0 + z)
    w2 = g_w * z / (1.0 + z)

    oh1 = lane == e1
    oh2 = lane == e2
    oh = jnp.where(oh1 | oh2, 1.0, 0.0).astype(BF16)
    r_i = lax.broadcasted_iota(jnp.int32, (tm, tm), 0)
    c_i = lax.broadcasted_iota(jnp.int32, (tm, tm), 1)
    below = jnp.where(c_i < r_i, 1.0, 0.0).astype(BF16)
    before = counts[0:1, :] + jnp.dot(below, oh, preferred_element_type=F32)
    rank1 = jnp.sum(jnp.where(oh1, before, 0.0), axis=-1, keepdims=True)
    rank2 = jnp.sum(jnp.where(oh2, before, 0.0), axis=-1, keepdims=True)
    counts[...] = counts[...] + jnp.sum(oh.astype(F32), axis=0, keepdims=True)
    cnt_ref[...] = counts[...]

    rec = jnp.zeros((tm, LANES), F32)
    for slot, val in ((R_E1, e1), (R_E2, e2), (R_W1, w1), (R_W2, w2), (R_RANK1, rank1), (R_RANK2, rank2)):
        rec = jnp.where(lane == slot, val, rec)
    rec_ref[...] = rec


def _router(lay, x_all, gain, mods, w_exp, b_exp, w_grp, b_grp, first_tile):
    t, d = x_all.shape
    assert d == ROW_CHUNKS * LANES
    n_rows = t - first_tile * MM_TILE
    pad = LANES - N_EXPERTS - N_GROUPS
    w = jnp.concatenate([w_exp, w_grp, jnp.zeros((d, pad), F32)], axis=1)
    b = jnp.concatenate([b_exp, b_grp, jnp.zeros((pad,), F32)]).reshape(1, LANES)
    w_hi = w.astype(BF16)
    w_lo = (w - w_hi.astype(F32)).astype(BF16)
    return pl.pallas_call(
        _router_kernel,
        grid=(n_rows // MM_TILE,),
        in_specs=[pl.BlockSpec((MM_TILE, d), lambda i: (i + first_tile, 0)),
                  pl.BlockSpec((1, d), lambda i: (0, 0)),
                  pl.BlockSpec((None, N_MOD, d), lambda i: (lay.mod_row_of_mm_tile(i + first_tile), 0, 0)),
                  pl.BlockSpec((d, LANES), lambda i: (0, 0)),
                  pl.BlockSpec((d, LANES), lambda i: (0, 0)),
                  pl.BlockSpec((1, LANES), lambda i: (0, 0))],
        out_specs=[pl.BlockSpec((MM_TILE * ROW_CHUNKS, LANES), lambda i: (i, 0)),
                   pl.BlockSpec((MM_TILE, LANES), lambda i: (i, 0)),
                   pl.BlockSpec((SUBLANES, LANES), lambda i: (0, 0))],
        out_shape=[jax.ShapeDtypeStruct((n_rows * ROW_CHUNKS, LANES), F32),
                   jax.ShapeDtypeStruct((n_rows, LANES), F32),
                   jax.ShapeDtypeStruct((SUBLANES, LANES), F32)],
        scratch_shapes=[pltpu.VMEM((SUBLANES, LANES), F32)],
        compiler_params=_cparams(1),
        name="moe_router",
    )(x_all, gain.reshape(1, d), mods, w_hi, w_lo, b)


def _token_copy(src_hbm, src_token, buf, dst_row, sem):
    src = pl.multiple_of(src_token * ROW_CHUNKS, ROW_CHUNKS)
    dst = dst_row * ROW_PITCH
    if not isinstance(dst, int):
        dst = pl.multiple_of(dst, SUBLANES)
    return pltpu.make_async_copy(src_hbm.at[pl.ds(src, ROW_CHUNKS), :], buf.at[pl.ds(dst, ROW_CHUNKS), :], sem)


def _row_copy(src_hbm, src_row, buf, dst_row, sem):
    return pltpu.make_async_copy(src_hbm.at[pl.ds(src_row, 1), :], buf.at[pl.ds(dst_row, 1), :], sem)


def _gathered(buf, rows, first_chunk, n_chunks):
    parts = [buf[pl.ds(j, rows, stride=ROW_PITCH), :] for j in range(first_chunk, first_chunk + n_chunks)]
    return parts[0] if n_chunks == 1 else jnp.concatenate(parts, axis=1)


def _expert_kernel(te_ref, src_ref, nt_ref, fl_hbm, wup_ref, wdn_ref, o_ref,
                   xbuf, sems, wup_bf, wdn_bf):
    i = pl.program_id(0)
    nt = nt_ref[0]
    slot = i % 2

    def start_gather(tile, slot_):
        def body(r, c):
            _token_copy(fl_hbm, src_ref[tile * EXP_TILE + r], xbuf.at[slot_], r, sems.at[slot_]).start()
            return c
        lax.fori_loop(0, EXP_TILE, body, 0, unroll=DMA_UNROLL)

    def wait_gather(slot_):
        for r in range(EXP_TILE):
            _token_copy(fl_hbm, 0, xbuf.at[slot_], r, sems.at[slot_]).wait()

    @pl.when(i == 0)
    def _():
        start_gather(0, 0)

    @pl.when(i < nt)
    def _():
        expert_changed = (i == 0) | (te_ref[i] != te_ref[jnp.maximum(i - 1, 0)])

        @pl.when(expert_changed)
        def _():
            wup_bf[...] = wup_ref[0].astype(BF16)
            wdn_bf[...] = wdn_ref[0].astype(BF16)

        wait_gather(slot)
        x = _gathered(xbuf.at[slot], EXP_TILE, 0, ROW_CHUNKS).astype(BF16)
        hu = jnp.dot(x, wup_bf[...], preferred_element_type=F32)
        h = (jax.nn.silu(hu[:, :D_EXPERT]) * hu[:, D_EXPERT:]).astype(BF16)
        part = ROW_CHUNKS // ISSUE_PARTS
        rows = EXP_TILE // ISSUE_PARTS
        for c in range(ISSUE_PARTS):
            cols = slice(c * part * LANES, (c + 1) * part * LANES)
            o_ref[:, cols] = jnp.dot(h, wdn_bf[:, cols], preferred_element_type=F32)
            for r in range(c * rows, (c + 1) * rows):
                _token_copy(fl_hbm, src_ref[(i + 1) * EXP_TILE + r], xbuf.at[1 - slot], r,
                          sems.at[1 - slot]).start()

    @pl.when(i == nt)
    def _():
        wait_gather(slot)

    @pl.when(i >= nt)
    def _():
        o_ref[...] = jnp.zeros_like(o_ref)


def _experts(fl, tile_expert, src_rows, n_tiles_used, w_up, w_down, n_tiles):
    d = w_up.shape[1]
    last = lambda i, nt: jnp.minimum(i, nt[0] - 1)
    grid_spec = pltpu.PrefetchScalarGridSpec(
        num_scalar_prefetch=3,
        grid=(n_tiles,),
        in_specs=[pl.BlockSpec(memory_space=pl.ANY),
                  pl.BlockSpec((1, d, 2 * D_EXPERT), lambda i, te, src, nt: (te[last(i, nt)], 0, 0)),
                  pl.BlockSpec((1, D_EXPERT, d), lambda i, te, src, nt: (te[last(i, nt)], 0, 0))],
        out_specs=pl.BlockSpec((EXP_TILE, d), lambda i, te, src, nt: (i, 0)),
        scratch_shapes=[pltpu.VMEM((2, EXP_TILE * ROW_PITCH, LANES), F32),
                        pltpu.SemaphoreType.DMA((2,)),
                        pltpu.VMEM((d, 2 * D_EXPERT), BF16),
                        pltpu.VMEM((D_EXPERT, d), BF16)])
    return pl.pallas_call(
        _expert_kernel,
        grid_spec=grid_spec,
        out_shape=jax.ShapeDtypeStruct((n_tiles * EXP_TILE, d), F32),
        compiler_params=_cparams(1),
        name="moe_experts",
    )(tile_expert, src_rows, n_tiles_used, fl, w_up, w_down)


def _combine_kernel(final, pos_ref, ys_hbm, x_ref, rec_ref, m_ref, gf_ref, o_ref, ybuf, sems):
    i = pl.program_id(0)
    n = pl.num_programs(0)
    slot = i % 2

    def start_gather(tile, slot_):
        def body(r, c):
            for k in range(2):
                _row_copy(ys_hbm, pos_ref[(tile * ROW_TILE + r) * 2 + k], ybuf.at[slot_, k], r,
                          sems.at[slot_]).start()
            return c
        lax.fori_loop(0, ROW_TILE, body, 0, unroll=DMA_UNROLL)

    def wait_gather(slot_):
        for r in range(2 * ROW_TILE):
            _row_copy(ys_hbm, 0, ybuf.at[slot_, r % 2], r // 2, sems.at[slot_]).wait()

    @pl.when(i == 0)
    def _():
        start_gather(0, 0)

    wait_gather(slot)
    rec = rec_ref[...]
    w1 = rec[:, R_W1:R_W1 + 1]
    w2 = rec[:, R_W2:R_W2 + 1]
    part = o_ref.shape[1] // ISSUE_PARTS
    toks = ROW_TILE // ISSUE_PARTS
    for c in range(ISSUE_PARTS):
        cols = slice(c * part, (c + 1) * part)
        y = w1 * ybuf[slot, 0, :, cols] + w2 * ybuf[slot, 1, :, cols]
        o_ref[:, cols] = x_ref[:, cols] + m_ref[5:6, cols] * y
        for r in range(c * toks, (c + 1) * toks):
            for k in range(2):
                _row_copy(ys_hbm, pos_ref[((i + 1) * ROW_TILE + r) * 2 + k], ybuf.at[1 - slot, k], r,
                          sems.at[1 - slot]).start()
    if final:
        x = o_ref[...]
        r = lax.rsqrt(jnp.mean(x * x, axis=-1, keepdims=True) + EPS)
        o_ref[...] = (x * r) * gf_ref[...]

    @pl.when(i == n - 1)
    def _():
        wait_gather(1 - slot)


def _combine(lay, pos, ys, x_all, rec, mods, gain_final, first_tile, final):
    d = x_all.shape[1]
    n_rows = rec.shape[0]
    off = first_tile * (MM_TILE // ROW_TILE)
    grid_spec = pltpu.PrefetchScalarGridSpec(
        num_scalar_prefetch=1,
        grid=(n_rows // ROW_TILE,),
        in_specs=[pl.BlockSpec(memory_space=pl.ANY),
                  pl.BlockSpec((ROW_TILE, d), lambda i, pos: (i + off, 0)),
                  pl.BlockSpec((ROW_TILE, LANES), lambda i, pos: (i, 0)),
                  pl.BlockSpec((None, N_MOD, d), lambda i, pos: (lay.mod_row_of_seq_tile(i + off), 0, 0)),
                  pl.BlockSpec((1, d), lambda i, pos: (0, 0))],
        out_specs=pl.BlockSpec((ROW_TILE, d), lambda i, pos: (i, 0)),
        scratch_shapes=[pltpu.VMEM((2, 2, ROW_TILE, d), F32),
                        pltpu.SemaphoreType.DMA((2,))])
    return pl.pallas_call(
        functools.partial(_combine_kernel, final),
        grid_spec=grid_spec,
        out_shape=jax.ShapeDtypeStruct((n_rows, d), F32),
        compiler_params=_cparams(1),
        name="moe_combine",
    )(jnp.concatenate([pos, jnp.zeros((2 * ROW_TILE,), jnp.int32)]), ys, x_all, rec, mods,
      gain_final.reshape(1, d))


def _dispatch_plan(rec, counts, n_tiles):
    n_rows = rec.shape[0]
    cnt = counts[0, :N_EXPERTS].astype(jnp.int32)
    tiles_per = (cnt + EXP_TILE - 1) // EXP_TILE
    tile_end = jnp.cumsum(tiles_per)
    offset = (tile_end - tiles_per) * EXP_TILE
    e_sel = rec[:, R_E1:R_E2 + 1].astype(jnp.int32)
    rank = rec[:, R_RANK1:R_RANK2 + 1].astype(jnp.int32)
    pos = (offset[e_sel] + rank).reshape(-1)
    token = jnp.repeat(jnp.arange(n_rows, dtype=jnp.int32), 2)
    src_rows = jnp.zeros((n_tiles * EXP_TILE,), jnp.int32).at[pos].set(token, unique_indices=True)
    n_used = tile_end[-1]
    tile_ids = jnp.minimum(jnp.arange(n_tiles, dtype=jnp.int32), n_used - 1)
    tile_expert = jnp.sum(tile_end[None, :] <= tile_ids[:, None], axis=1).astype(jnp.int32)
    return pos, src_rows, tile_expert, n_used.reshape(1).astype(jnp.int32)


def _moe(lay, x_all, gain, mods, w_grp, b_grp, w_exp, b_exp, w_up_all, w_down_all, layer, gain_final,
         first_tile, final):
    fl, rec, counts = _router(lay, x_all, gain, mods, w_exp, b_exp, w_grp, b_grp, first_tile)
    n_rows = rec.shape[0]
    assert (2 * n_rows) % EXP_TILE == 0
    n_tiles = (2 * n_rows) // EXP_TILE + N_EXPERTS
    pos, src_rows, tile_expert, n_used = _dispatch_plan(rec, counts, n_tiles)
    d = x_all.shape[1]
    w_up = w_up_all.reshape(-1, d, 2 * D_EXPERT)
    w_down = w_down_all.reshape(-1, D_EXPERT, d)
    ys = _experts(fl, tile_expert + layer * N_EXPERTS, src_rows, n_used, w_up, w_down, n_tiles)
    return _combine(lay, pos, ys, x_all, rec, mods, gain_final, first_tile, final)


def kernel(x, c, ctx, c_ctx, mod_w, mod_b, norm_mix, norm_ffn, w_in, w_out, rg_conv_w, rg_conv_b,
           rg_gate_w, rg_gate_b, rg_lam, sc_conv_w, pool_w, pool_b, pool_scale, da_lam, da_norm,
           moe_grp_w, moe_grp_b, moe_exp_w, moe_exp_b, moe_up, moe_down, norm_final):
    n_batch, seq, d = x.shape
    n_ctx = ctx.shape[1]
    depth = mod_w.shape[0]
    lay = _Layout(n_batch, n_ctx, seq)
    assert 1 + n_batch <= SUBLANES

    cc = jnp.zeros((SUBLANES, d), F32).at[0].set(c_ctx).at[1:1 + n_batch].set(c)
    mods_all = _modulation(cc, mod_w, mod_b)
    cos, sin = _rope_tables(lay)
    x_all = (ctx.reshape(n_batch * n_ctx, d), x.reshape(n_batch * seq, d))

    for l in range(depth):
        last = l == depth - 1
        lam_init = 0.8 - 0.6 * math.exp(-0.3 * l)
        mods = mods_all[l]

        proj = _in_projection(lay, x_all, norm_mix[l], mods, w_in[l].astype(BF16))
        hf = _rglru(lay, proj, rg_conv_w[l], rg_conv_b[l], rg_gate_w[l, 0], rg_gate_b[l, 0],
                    rg_lam[l, 0], reverse=False)
        ya = _rglru(lay, proj, rg_conv_w[l], rg_conv_b[l], rg_gate_w[l, 1], rg_gate_b[l, 1],
                    rg_lam[l, 1], reverse=True, hf=hf)
        yb, yc = _conv_pool(lay, proj, sc_conv_w[l], pool_w[l], pool_b[l], pool_scale[l])
        qs, kr, vt = _rope(lay, proj, cos, sin)
        yd = _attention(lay, qs, kr, vt, da_lam[l], da_norm[l], lam_init)
        x_all = _out_projection(lay, (ya, yb, yc, yd), w_out[l].astype(BF16), x_all, mods)

        first_tile = 1 if last else 0
        out = _moe(lay, x_all, norm_ffn[l], mods, moe_grp_w[l], moe_grp_b[l], moe_exp_w[l], moe_exp_b[l],
                   moe_up, moe_down, l, norm_final, first_tile, last)
        if last:
            return out.reshape(n_batch, seq, d)
        x_all = out
```

```python
import functools
import math

import jax
import jax.numpy as jnp
from jax import lax
from jax.experimental import pallas as pl
from jax.experimental.pallas import tpu as pltpu

F32 = jnp.float32
BF16 = jnp.bfloat16

W_GROUP = 512
N_PROJ = 9
RG_BLOCKS = 4
RG_BW = 128
RG_C = 8.0
POOL_WINDOWS = (2, 4, 8, 16)
POOL_GW = 128
DA_HEADS = 4
DA_HD = 64
DA_VD = 128
ROPE_THETA = 10000.0
GRID_W = 64
N_GROUPS = 4
EXP_PER_GROUP = 8
N_EXPERTS = 32
D_EXPERT = 512
N_MOD = 6
EPS = 1e-6

SUBLANES = 8
LANES = 128
ROW_TILE = 256
MM_TILE = 512
MAX_KV_CHUNK = 2816
ONES_ROWS = 16
EXP_TILE = 256
DMA_UNROLL = 8
ISSUE_PARTS = 8
ROW_CHUNKS = 16
ROW_PITCH = 24
VMEM_LIMIT = 56 * 1024 * 1024


def _cparams(n_axes):
    return pltpu.CompilerParams(dimension_semantics=("arbitrary",) * n_axes,
                                vmem_limit_bytes=VMEM_LIMIT)


def _mod_kernel(c_ref, w_ref, b_ref, o_ref):
    c = c_ref[...]
    s = c * jax.nn.sigmoid(c)
    o_ref[0] = jnp.dot(s.astype(BF16), w_ref[0].astype(BF16), preferred_element_type=F32) + b_ref[0]


def _modulation(cc, mod_w, mod_b):
    depth, d, n = mod_w.shape
    tn = 1024
    out = pl.pallas_call(
        _mod_kernel,
        grid=(depth, n // tn),
        in_specs=[pl.BlockSpec((SUBLANES, d), lambda l, j: (0, 0)),
                  pl.BlockSpec((1, d, tn), lambda l, j: (l, 0, j)),
                  pl.BlockSpec((1, 1, tn), lambda l, j: (l, 0, j))],
        out_specs=pl.BlockSpec((1, SUBLANES, tn), lambda l, j: (l, 0, j)),
        out_shape=jax.ShapeDtypeStruct((depth, SUBLANES, n), F32),
        compiler_params=_cparams(2),
        name="modulation",
    )(cc, mod_w, mod_b.reshape(depth, 1, n))
    return out.reshape(depth, SUBLANES, N_MOD, d)


def _rms_mod(x, g, shift, scale):
    r = lax.rsqrt(jnp.mean(x * x, axis=-1, keepdims=True) + EPS)
    return (x * r) * g * (1.0 + scale) + shift


class _Layout:
    def __init__(self, n_batch, n_ctx, seq):
        assert n_ctx == ROW_TILE, "one sequence tile per context sequence"
        assert n_batch * n_ctx == MM_TILE, "context rows fill exactly one projection tile"
        assert seq % MM_TILE == 0 and seq % GRID_W == 0
        self.b = n_batch
        self.n_ctx = n_ctx
        self.seq = seq
        self.rows = n_batch * (n_ctx + seq)
        self.nl = seq // ROW_TILE
        self.n_tiles = self.rows // ROW_TILE

    def mod_row_of_seq_tile(self, i):
        return jnp.where(i < self.b, 0, 1 + (i - self.b) // self.nl)

    def mod_row_of_mm_tile(self, i):
        return jnp.where(i < 1, 0, 1 + (i - 1) // (self.seq // MM_TILE))


def _row_tile_specs(x_src, tile_of):
    if not isinstance(x_src, tuple):
        return [pl.BlockSpec((MM_TILE, x_src.shape[1]), lambda *ids: (tile_of(*ids), 0))]
    ctx_rows, lat_rows = x_src
    assert ctx_rows.shape[0] == MM_TILE
    d = ctx_rows.shape[1]
    return [pl.BlockSpec((MM_TILE, d), lambda *ids: (0, 0)),
            pl.BlockSpec((MM_TILE, d), lambda *ids: (jnp.maximum(tile_of(*ids) - 1, 0), 0))]


def _row_tile(refs, tile):
    if len(refs) == 1:
        return refs[0][...]
    return jnp.where(tile == 0, refs[0][...], refs[1][...])


def _src_list(x_src):
    return list(x_src) if isinstance(x_src, tuple) else [x_src]


def _inproj_kernel(n_src, *refs):
    x_refs, (g_ref, m_ref, w_ref, o_ref) = refs[:n_src], refs[n_src:]
    h = _rms_mod(_row_tile(x_refs, pl.program_id(1)), g_ref[...], m_ref[0:1, :], m_ref[1:2, :])
    o_ref[...] = jnp.dot(h.astype(BF16), w_ref[...], preferred_element_type=F32)


def _in_projection(lay, x_src, gain, mods, w_bf):
    xs = _src_list(x_src)
    t, d = lay.rows, xs[0].shape[1]
    n = w_bf.shape[1]
    tn = n // 2
    return pl.pallas_call(
        functools.partial(_inproj_kernel, len(xs)),
        grid=(n // tn, t // MM_TILE),
        in_specs=_row_tile_specs(x_src, lambda j, i: i) + [
                  pl.BlockSpec((1, d), lambda j, i: (0, 0)),
                  pl.BlockSpec((None, N_MOD, d), lambda j, i: (lay.mod_row_of_mm_tile(i), 0, 0)),
                  pl.BlockSpec((d, tn), lambda j, i: (0, j))],
        out_specs=pl.BlockSpec((MM_TILE, tn), lambda j, i: (i, j)),
        out_shape=jax.ShapeDtypeStruct((t, n), F32),
        compiler_params=_cparams(2),
        name="in_projection",
    )(*xs, gain.reshape(1, d), mods, w_bf)


def _fill_ext(ext, prev_ref, x, next_ref, is_first, is_last):
    ext[0:SUBLANES, :] = jnp.where(is_first, 0.0, prev_ref[...])
    ext[SUBLANES:SUBLANES + ROW_TILE, :] = x
    ext[SUBLANES + ROW_TILE:, :] = jnp.where(is_last, 0.0, next_ref[...])


def _shifted(ext, off, cols=slice(None)):
    return ext[SUBLANES + off:SUBLANES + off + ROW_TILE, cols]


def _rglru_kernel(reverse, n_lat_tiles, *refs):
    if reverse:
        (x_ref, xp_ref, xn_ref, cw_ref, cb_ref, gw_ref, gb_ref, lam_ref, hf_ref, ag_ref,
         o_ref, ext, a_scr, b_scr, carry) = refs
    else:
        (x_ref, xp_ref, xn_ref, cw_ref, cb_ref, gw_ref, gb_ref, lam_ref,
         o_ref, ext, a_scr, b_scr, carry) = refs
    s = pl.program_id(1)
    if reverse:
        is_first = (s == 0) | (s == n_lat_tiles)
        is_last = (s == 0) | (s == 1)
    else:
        is_first = (s == 0) | (s == 1)
        is_last = (s == 0) | (s == n_lat_tiles)

    @pl.when(s == 0)
    def _():
        carry[...] = jnp.zeros_like(carry)

    _fill_ext(ext, xp_ref, x_ref[...], xn_ref, is_first, is_last)
    u = cb_ref[...] + sum(cw_ref[k:k + 1, :] * _shifted(ext, k - 2) for k in range(4))

    sp = -lam_ref[...]
    sp = jnp.maximum(sp, 0.0) + jnp.log1p(jnp.exp(-jnp.abs(sp)))
    for n in range(RG_BLOCKS):
        cols = slice(n * RG_BW, (n + 1) * RG_BW)
        un = u[:, cols]
        g = jnp.dot(un.astype(BF16), gw_ref[n], preferred_element_type=F32)
        r = jax.nn.sigmoid(g[:, :RG_BW] + gb_ref[0:1, cols])
        gi = jax.nn.sigmoid(g[:, RG_BW:] + gb_ref[1:2, cols])
        log_a = -RG_C * r * sp[:, cols]
        a_scr[:, cols] = jnp.exp(log_a)
        th = jnp.tanh(log_a)
        b_scr[:, cols] = jnp.sqrt(-2.0 * th / (1.0 - th)) * (gi * un)

    row = lax.broadcasted_iota(jnp.int32, (SUBLANES, W_GROUP), 0)
    n_groups = ROW_TILE // SUBLANES

    def group(gidx, h_prev):
        g0 = (n_groups - 1 - gidx) if reverse else gidx
        start = pl.multiple_of(g0 * SUBLANES, SUBLANES)
        a8 = a_scr[pl.ds(start, SUBLANES), :]
        b8 = b_scr[pl.ds(start, SUBLANES), :]
        for sh in (1, 2, 4):
            rot = (SUBLANES - sh) if reverse else sh
            a_sh = pltpu.roll(a8, rot, 0)
            b_sh = pltpu.roll(b8, rot, 0)
            valid = (row < SUBLANES - sh) if reverse else (row >= sh)
            b8 = jnp.where(valid, a8 * b_sh + b8, b8)
            a8 = jnp.where(valid, a8 * a_sh, a8)
        h8 = a8 * h_prev + b8
        a_scr[pl.ds(start, SUBLANES), :] = h8
        edge = 0 if reverse else SUBLANES - 1
        return h8[edge:edge + 1, :]

    carry[...] = lax.fori_loop(0, n_groups, group, carry[...], unroll=4)

    h = a_scr[...]
    if reverse:
        o_ref[...] = (jax.nn.gelu(ag_ref[...]) * (hf_ref[...] + h)).astype(o_ref.dtype)
    else:
        o_ref[...] = h


def _rglru(lay, proj, conv_w, conv_b, gate_w, gate_b, lam, reverse, hf=None):
    t = proj.shape[0]
    nl = lay.nl
    nb8 = ROW_TILE // SUBLANES

    def tile(b, s):
        lat = (nl - s) if reverse else (s - 1)
        return jnp.where(s == 0, b, lay.b + b * nl + lat)

    main = lambda c: pl.BlockSpec((ROW_TILE, W_GROUP), lambda b, s: (tile(b, s), c))
    prev = pl.BlockSpec((SUBLANES, W_GROUP), lambda b, s: (jnp.maximum(tile(b, s) * nb8 - 1, 0), 0))
    nxt = pl.BlockSpec((SUBLANES, W_GROUP),
                       lambda b, s: (jnp.minimum((tile(b, s) + 1) * nb8, t // SUBLANES - 1), 0))
    const = lambda shape: pl.BlockSpec(shape, lambda b, s: (0,) * len(shape))
    gw = jnp.transpose(gate_w, (1, 2, 0, 3)).reshape(RG_BLOCKS, RG_BW, 2 * RG_BW).astype(BF16)
    gb = gate_b.reshape(2, W_GROUP)
    in_specs = [main(0), prev, nxt, const((4, W_GROUP)), const((1, W_GROUP)),
                const((RG_BLOCKS, RG_BW, 2 * RG_BW)), const((2, W_GROUP)), const((1, W_GROUP))]
    args = [proj, proj, proj, conv_w, conv_b.reshape(1, W_GROUP), gw, gb, lam.reshape(1, W_GROUP)]
    if reverse:
        in_specs += [pl.BlockSpec((ROW_TILE, W_GROUP), lambda b, s: (tile(b, s), 0)), main(1)]
        args += [hf, proj]
    return pl.pallas_call(
        functools.partial(_rglru_kernel, reverse, nl),
        grid=(lay.b, 1 + nl),
        in_specs=in_specs,
        out_specs=pl.BlockSpec((ROW_TILE, W_GROUP), lambda b, s: (tile(b, s), 0)),
        out_shape=jax.ShapeDtypeStruct((t, W_GROUP), BF16 if reverse else F32),
        scratch_shapes=[pltpu.VMEM((ROW_TILE + 2 * SUBLANES, W_GROUP), F32),
                        pltpu.VMEM((ROW_TILE, W_GROUP), F32),
                        pltpu.VMEM((ROW_TILE, W_GROUP), F32),
                        pltpu.VMEM((1, W_GROUP), F32)],
        compiler_params=_cparams(2),
        name="rglru_bwd" if reverse else "rglru_fwd",
    )(*args)


def _conv_pool_kernel(lay, bb_ref, bc_ref, bcp_ref, bcn_ref, bx_ref, bxp_ref, bxn_ref,
                      cx_ref, cxp_ref, cxn_ref, scw_ref, pw_ref, pb_ref, ps_ref,
                      yb_ref, yc_ref, ext):
    i = pl.program_id(0)
    j = jnp.maximum(i - lay.b, 0) % lay.nl
    is_ctx = i < lay.b
    is_first = is_ctx | (j == 0)
    is_last = is_ctx | (j == lay.nl - 1)

    ext[0:SUBLANES, :] = jnp.where(is_first, 0.0, bcp_ref[...] * bxp_ref[...])
    ext[SUBLANES:SUBLANES + ROW_TILE, :] = bc_ref[...] * bx_ref[...]
    ext[SUBLANES + ROW_TILE:, :] = jnp.where(is_last, 0.0, bcn_ref[...] * bxn_ref[...])
    conv = sum(scw_ref[k:k + 1, :] * _shifted(ext, k - 1) for k in range(3))
    yb_ref[...] = (bb_ref[...] * conv).astype(yb_ref.dtype)

    x = cx_ref[...]
    _fill_ext(ext, cxp_ref, x, cxn_ref, is_first, is_last)
    seq_len = jnp.where(is_ctx, lay.n_ctx, lay.seq)
    pos = jnp.where(is_ctx, 0, j * ROW_TILE) + lax.broadcasted_iota(jnp.int32, (ROW_TILE, 1), 0)
    for gi, win in enumerate(POOL_WINDOWS):
        cols = slice(gi * POOL_GW, (gi + 1) * POOL_GW)
        half = win // 2
        tot = sum(_shifted(ext, off, cols) for off in range(-half, half))
        cnt = jnp.minimum(pos + half, seq_len) - jnp.maximum(pos - half, 0)
        dlt = tot / cnt.astype(F32) - x[:, cols]
        y = jnp.dot(dlt.astype(BF16), pw_ref[gi].astype(BF16), preferred_element_type=F32)
        yc_ref[:, cols] = ((y + pb_ref[gi:gi + 1, :]) * ps_ref[:, cols]).astype(yc_ref.dtype)


def _conv_pool(lay, proj, sc_w, pool_w, pool_b, pool_scale):
    t = proj.shape[0]
    nb8 = ROW_TILE // SUBLANES
    main = lambda c: pl.BlockSpec((ROW_TILE, W_GROUP), lambda i: (i, c))
    prev = lambda c: pl.BlockSpec((SUBLANES, W_GROUP), lambda i: (jnp.maximum(i * nb8 - 1, 0), c))
    nxt = lambda c: pl.BlockSpec((SUBLANES, W_GROUP),
                                 lambda i: (jnp.minimum((i + 1) * nb8, t // SUBLANES - 1), c))
    const = lambda shape: pl.BlockSpec(shape, lambda i: (0,) * len(shape))
    n_pool = len(POOL_WINDOWS)
    out = jax.ShapeDtypeStruct((t, W_GROUP), BF16)
    return pl.pallas_call(
        functools.partial(_conv_pool_kernel, lay),
        grid=(t // ROW_TILE,),
        in_specs=[main(2), main(3), prev(3), nxt(3), main(4), prev(4), nxt(4),
                  main(5), prev(5), nxt(5), const((3, W_GROUP)),
                  const((n_pool, POOL_GW, POOL_GW)), const((n_pool, POOL_GW)), const((1, W_GROUP))],
        out_specs=[pl.BlockSpec((ROW_TILE, W_GROUP), lambda i: (i, 0))] * 2,
        out_shape=[out, out],
        scratch_shapes=[pltpu.VMEM((ROW_TILE + 2 * SUBLANES, W_GROUP), F32)],
        compiler_params=_cparams(1),
        name="conv_pool",
    )(*([proj] * 10), sc_w, pool_w, pool_b, pool_scale.reshape(1, W_GROUP))


def _rope_tables(lay):
    nf = DA_HD // 4
    inv = ROPE_THETA ** (-jnp.arange(nf, dtype=F32) / nf)
    pos = jnp.arange(lay.seq, dtype=jnp.int32)
    ang_r = (pos // GRID_W).astype(F32)[:, None] * inv
    ang_c = (pos % GRID_W).astype(F32)[:, None] * inv
    cos = jnp.concatenate([jnp.cos(ang_r)] * 2 + [jnp.cos(ang_c)] * 2, axis=-1)
    sin = jnp.concatenate([-jnp.sin(ang_r), jnp.sin(ang_r), -jnp.sin(ang_c), jnp.sin(ang_c)], axis=-1)
    cos = jnp.concatenate([jnp.ones((ROW_TILE, DA_HD), F32), cos], axis=0)
    sin = jnp.concatenate([jnp.zeros((ROW_TILE, DA_HD), F32), sin], axis=0)
    return jnp.tile(cos, (1, 2)), jnp.tile(sin, (1, 2))


def _rope_kernel(q_ref, k_ref, v_ref, cos_ref, sin_ref, qo_ref, ko_ref, vto_ref):
    cos = cos_ref[...]
    sin = sin_ref[...]
    nf = DA_HD // 4
    lane = lax.broadcasted_iota(jnp.int32, (ROW_TILE, LANES), 1)
    first_half = (lane & (2 * nf - 1)) < nf

    def rot(x):
        partner = jnp.where(first_half, pltpu.roll(x, LANES - nf, 1), pltpu.roll(x, nf, 1))
        return x * cos + partner * sin

    for h in range(DA_HEADS):
        cols = slice(h * DA_VD, (h + 1) * DA_VD)
        qo_ref[:, cols] = (rot(q_ref[:, cols]) * (DA_HD ** -0.5)).astype(BF16)
        ko_ref[:, cols] = rot(k_ref[:, cols]).astype(BF16)
        vto_ref[cols, :] = v_ref[:, cols].T.astype(BF16)


def _rope(lay, proj, cos, sin):
    t = proj.shape[0]
    nl = lay.nl
    main = lambda c: pl.BlockSpec((ROW_TILE, W_GROUP), lambda i: (i, c))
    tab = pl.BlockSpec((ROW_TILE, LANES), lambda i: (jnp.where(i < lay.b, 0, 1 + (i - lay.b) % nl), 0))

    def key_tile(i):
        j = jnp.maximum(i - lay.b, 0)
        return jnp.where(i < lay.b, i * (nl + 1), (j // nl) * (nl + 1) + 1 + j % nl)

    return pl.pallas_call(
        _rope_kernel,
        grid=(t // ROW_TILE,),
        in_specs=[main(6), main(7), main(8), tab, tab],
        out_specs=[pl.BlockSpec((ROW_TILE, W_GROUP), lambda i: (i, 0)),
                   pl.BlockSpec((ROW_TILE, W_GROUP), lambda i: (key_tile(i), 0)),
                   pl.BlockSpec((W_GROUP, ROW_TILE), lambda i: (0, key_tile(i)))],
        out_shape=[jax.ShapeDtypeStruct((t, W_GROUP), BF16),
                   jax.ShapeDtypeStruct((t, W_GROUP), BF16),
                   jax.ShapeDtypeStruct((W_GROUP, t), BF16)],
        compiler_params=_cparams(1),
        name="rope",
    )(proj, proj, proj, cos, sin)


def _kv_chunk(n_keys):
    return max(c for c in range(ROW_TILE, MAX_KV_CHUNK + 1, ROW_TILE) if n_keys % c == 0)


def _attn_kernel(lam_init, n_ctx, chunk, q_ref, k_ref, vt_ref, lp_ref, g_ref, o_ref, qqt, m_scr, acc):
    qi = pl.program_id(2)
    tq = q_ref.shape[0]
    n_keys = k_ref.shape[0]

    qt = q_ref[...].astype(F32).T
    feat = lax.broadcasted_iota(jnp.int32, qt.shape, 0)
    qqt[:, 0:tq] = jnp.where(feat < DA_HD, qt, 0.0).astype(BF16)
    qqt[:, tq:] = jnp.where(feat >= DA_HD, qt, 0.0).astype(BF16)
    m_scr[...] = jnp.full_like(m_scr, -jnp.inf)
    acc[...] = jnp.zeros_like(acc)

    def scores(start, size):
        s = jnp.dot(k_ref[start:start + size, :], qqt[...], preferred_element_type=F32)
        return s, jnp.max(s, axis=0, keepdims=True)

    def accumulate(start, size, s, s_max):
        m_prev = m_scr[0:1, :]
        m_new = jnp.maximum(m_prev, s_max)
        alpha = jnp.exp(m_prev - m_new)
        p = jnp.exp(s - m_new).astype(BF16)
        vt = jnp.concatenate([vt_ref[:, start:start + size], jnp.ones((ONES_ROWS, size), BF16)], axis=0)
        acc[...] = alpha * acc[...] + jnp.dot(vt, p, preferred_element_type=F32)
        m_scr[...] = jnp.broadcast_to(m_new, m_scr.shape)

    @pl.when(qi == 0)
    def _():
        accumulate(0, n_ctx, *scores(0, n_ctx))

    @pl.when(qi > 0)
    def _():
        n_chunks = n_keys // chunk
        cur = scores(0, chunk)
        for c in range(n_chunks):
            nxt = scores((c + 1) * chunk, chunk) if c + 1 < n_chunks else None
            accumulate(c * chunk, chunk, *cur)
            cur = nxt

    lp = lp_ref[...]
    lam = (jnp.exp(jnp.sum(lp[0:1] * lp[1:2], axis=-1, keepdims=True))
           - jnp.exp(jnp.sum(lp[2:3] * lp[3:4], axis=-1, keepdims=True)) + lam_init)
    den = acc[DA_VD:DA_VD + 1, :]
    ot = acc[0:DA_VD, 0:tq] / den[:, 0:tq] - lam * (acc[0:DA_VD, tq:] / den[:, tq:])
    o = ot.T
    r = lax.rsqrt(jnp.mean(o * o, axis=-1, keepdims=True) + EPS)
    o_ref[...] = ((o * r) * g_ref[...] * (1.0 - lam_init)).astype(o_ref.dtype)


def _attention(lay, qs, kr, vt, da_lam, da_norm, lam_init):
    t = qs.shape[0]
    nl = lay.nl
    tq = ROW_TILE
    n_keys = lay.n_ctx + lay.seq
    q_spec = pl.BlockSpec((tq, DA_VD), lambda b, h, qi: (jnp.where(qi == 0, b, lay.b + b * nl + qi - 1), h))
    const = lambda shape: pl.BlockSpec(shape, lambda b, h, qi: (0,) * len(shape))
    return pl.pallas_call(
        functools.partial(_attn_kernel, lam_init, lay.n_ctx, _kv_chunk(n_keys)),
        grid=(lay.b, DA_HEADS, 1 + nl),
        in_specs=[q_spec,
                  pl.BlockSpec((n_keys, DA_VD), lambda b, h, qi: (b, h)),
                  pl.BlockSpec((DA_VD, n_keys), lambda b, h, qi: (h, b)),
                  const((4, DA_HD)), const((1, DA_VD))],
        out_specs=q_spec,
        out_shape=jax.ShapeDtypeStruct((t, W_GROUP), BF16),
        scratch_shapes=[pltpu.VMEM((DA_VD, 2 * tq), BF16),
                        pltpu.VMEM((SUBLANES, 2 * tq), F32),
                        pltpu.VMEM((DA_VD + ONES_ROWS, 2 * tq), F32)],
        compiler_params=_cparams(3),
        name="attention",
    )(qs, kr, vt, da_lam, da_norm.reshape(1, DA_VD))


def _outproj_kernel(n_src, ya_ref, yb_ref, yc_ref, yd_ref, w_ref, m_ref, *refs):
    x_refs, o_ref = refs[:n_src], refs[n_src]
    y = jnp.concatenate([ya_ref[...], yb_ref[...], yc_ref[...], yd_ref[...]], axis=-1)
    x = _row_tile(x_refs, pl.program_id(0))
    o_ref[...] = x + m_ref[2:3, :] * jnp.dot(y, w_ref[...], preferred_element_type=F32)


def _out_projection(lay, ys, w_bf, x_src, mods):
    xs = _src_list(x_src)
    t, d = lay.rows, xs[0].shape[1]
    y_spec = pl.BlockSpec((MM_TILE, W_GROUP), lambda i: (i, 0))
    return pl.pallas_call(
        functools.partial(_outproj_kernel, len(xs)),
        grid=(t // MM_TILE,),
        in_specs=[y_spec] * 4 + [
            pl.BlockSpec((d, d), lambda i: (0, 0)),
            pl.BlockSpec((None, N_MOD, d), lambda i: (lay.mod_row_of_mm_tile(i), 0, 0))]
            + _row_tile_specs(x_src, lambda i: i),
        out_specs=pl.BlockSpec((MM_TILE, d), lambda i: (i, 0)),
        out_shape=jax.ShapeDtypeStruct((t, d), F32),
        compiler_params=_cparams(1),
        name="out_projection",
    )(*ys, w_bf, mods, *xs)


R_E1, R_E2, R_W1, R_W2, R_RANK1, R_RANK2 = range(6)


def _split_store(ref, value):
    rows = value.shape[0]
    for j in range(value.shape[1] // LANES):
        ref[pl.ds(j, rows, stride=ROW_CHUNKS), :] = value[:, j * LANES:(j + 1) * LANES]


def _router_kernel(x_ref, g_ref, m_ref, whi_ref, wlo_ref, b_ref, fl_ref, rec_ref, cnt_ref, counts):
    i = pl.program_id(0)

    @pl.when(i == 0)
    def _():
        counts[...] = jnp.zeros_like(counts)

    fl = _rms_mod(x_ref[...], g_ref[...], m_ref[3:4, :], m_ref[4:5, :])
    _split_store(fl_ref, fl)
    fl_hi = fl.astype(BF16)
    fl_lo = (fl - fl_hi.astype(F32)).astype(BF16)
    dot = functools.partial(jnp.dot, preferred_element_type=F32)
    logit = (dot(fl_hi, whi_ref[...]) + (dot(fl_hi, wlo_ref[...]) + dot(fl_lo, whi_ref[...]))) + b_ref[...]
    tm = logit.shape[0]
    lane = lax.broadcasted_iota(jnp.int32, (tm, LANES), 1).astype(F32)
    neg = -jnp.inf

    def first_argmax(vals):
        top = jnp.max(vals, axis=-1, keepdims=True)
        return top, jnp.min(jnp.where(vals == top, lane, float(LANES)), axis=-1, keepdims=True)

    is_grp = (lane >= N_EXPERTS) & (lane < N_EXPERTS + N_GROUPS)
    g_top, g_lane = first_argmax(jnp.where(is_grp, logit, neg))
    g_w = 1.0 / jnp.sum(jnp.where(is_grp, jnp.exp(logit - g_top), 0.0), axis=-1, keepdims=True)

    grp_start = (g_lane - N_EXPERTS) * EXP_PER_GROUP
    in_grp = (lane >= grp_start) & (lane < grp_start + EXP_PER_GROUP)
    v1, e1 = first_argmax(jnp.where(in_grp, logit, neg))
    v2, e2 = first_argmax(jnp.where(in_grp & (lane != e1), logit, neg))
    z = jnp.exp(v2 - v1)
    w1 = g_w / (1.0 + z)
    w2 = g_w * z / (1.0 + z)

    oh1 = lane == e1
    oh2 = lane == e2
    oh = jnp.where(oh1 | oh2, 1.0, 0.0).astype(BF16)
    r_i = lax.broadcasted_iota(jnp.int32, (tm, tm), 0)
    c_i = lax.broadcasted_iota(jnp.int32, (tm, tm), 1)
    below = jnp.where(c_i < r_i, 1.0, 0.0).astype(BF16)
    before = counts[0:1, :] + jnp.dot(below, oh, preferred_element_type=F32)
    rank1 = jnp.sum(jnp.where(oh1, before, 0.0), axis=-1, keepdims=True)
    rank2 = jnp.sum(jnp.where(oh2, before, 0.0), axis=-1, keepdims=True)
    counts[...] = counts[...] + jnp.sum(oh.astype(F32), axis=0, keepdims=True)
    cnt_ref[...] = counts[...]

    rec = jnp.zeros((tm, LANES), F32)
    for slot, val in ((R_E1, e1), (R_E2, e2), (R_W1, w1), (R_W2, w2), (R_RANK1, rank1), (R_RANK2, rank2)):
        rec = jnp.where(lane == slot, val, rec)
    rec_ref[...] = rec


def _router(lay, x_all, gain, mods, w_exp, b_exp, w_grp, b_grp, first_tile):
    t, d = x_all.shape
    assert d == ROW_CHUNKS * LANES
    n_rows = t - first_tile * MM_TILE
    pad = LANES - N_EXPERTS - N_GROUPS
    w = jnp.concatenate([w_exp, w_grp, jnp.zeros((d, pad), F32)], axis=1)
    b = jnp.concatenate([b_exp, b_grp, jnp.zeros((pad,), F32)]).reshape(1, LANES)
    w_hi = w.astype(BF16)
    w_lo = (w - w_hi.astype(F32)).astype(BF16)
    return pl.pallas_call(
        _router_kernel,
        grid=(n_rows // MM_TILE,),
        in_specs=[pl.BlockSpec((MM_TILE, d), lambda i: (i + first_tile, 0)),
                  pl.BlockSpec((1, d), lambda i: (0, 0)),
                  pl.BlockSpec((None, N_MOD, d), lambda i: (lay.mod_row_of_mm_tile(i + first_tile), 0, 0)),
                  pl.BlockSpec((d, LANES), lambda i: (0, 0)),
                  pl.BlockSpec((d, LANES), lambda i: (0, 0)),
                  pl.BlockSpec((1, LANES), lambda i: (0, 0))],
        out_specs=[pl.BlockSpec((MM_TILE * ROW_CHUNKS, LANES), lambda i: (i, 0)),
                   pl.BlockSpec((MM_TILE, LANES), lambda i: (i, 0)),
                   pl.BlockSpec((SUBLANES, LANES), lambda i: (0, 0))],
        out_shape=[jax.ShapeDtypeStruct((n_rows * ROW_CHUNKS, LANES), F32),
                   jax.ShapeDtypeStruct((n_rows, LANES), F32),
                   jax.ShapeDtypeStruct((SUBLANES, LANES), F32)],
        scratch_shapes=[pltpu.VMEM((SUBLANES, LANES), F32)],
        compiler_params=_cparams(1),
        name="moe_router",
    )(x_all, gain.reshape(1, d), mods, w_hi, w_lo, b)


def _token_copy(src_hbm, src_token, buf, dst_row, sem):
    src = pl.multiple_of(src_token * ROW_CHUNKS, ROW_CHUNKS)
    dst = dst_row * ROW_PITCH
    if not isinstance(dst, int):
        dst = pl.multiple_of(dst, SUBLANES)
    return pltpu.make_async_copy(src_hbm.at[pl.ds(src, ROW_CHUNKS), :], buf.at[pl.ds(dst, ROW_CHUNKS), :], sem)


def _row_copy(src_hbm, src_row, buf, dst_row, sem):
    return pltpu.make_async_copy(src_hbm.at[pl.ds(src_row, 1), :], buf.at[pl.ds(dst_row, 1), :], sem)


def _gathered(buf, rows, first_chunk, n_chunks):
    parts = [buf[pl.ds(j, rows, stride=ROW_PITCH), :] for j in range(first_chunk, first_chunk + n_chunks)]
    return parts[0] if n_chunks == 1 else jnp.concatenate(parts, axis=1)


def _expert_kernel(te_ref, src_ref, nt_ref, fl_hbm, wup_ref, wdn_ref, o_ref,
                   xbuf, sems, wup_bf, wdn_bf):
    i = pl.program_id(0)
    nt = nt_ref[0]
    slot = i % 2

    def start_gather(tile, slot_):
        def body(r, c):
            _token_copy(fl_hbm, src_ref[tile * EXP_TILE + r], xbuf.at[slot_], r, sems.at[slot_]).start()
            return c
        lax.fori_loop(0, EXP_TILE, body, 0, unroll=DMA_UNROLL)

    def wait_gather(slot_):
        for r in range(EXP_TILE):
            _token_copy(fl_hbm, 0, xbuf.at[slot_], r, sems.at[slot_]).wait()

    @pl.when(i == 0)
    def _():
        start_gather(0, 0)

    @pl.when(i < nt)
    def _():
        expert_changed = (i == 0) | (te_ref[i] != te_ref[jnp.maximum(i - 1, 0)])

        @pl.when(expert_changed)
        def _():
            wup_bf[...] = wup_ref[0].astype(BF16)
            wdn_bf[...] = wdn_ref[0].astype(BF16)

        wait_gather(slot)
        x = _gathered(xbuf.at[slot], EXP_TILE, 0, ROW_CHUNKS).astype(BF16)
        hu = jnp.dot(x, wup_bf[...], preferred_element_type=F32)
        h = (jax.nn.silu(hu[:, :D_EXPERT]) * hu[:, D_EXPERT:]).astype(BF16)
        part = ROW_CHUNKS // ISSUE_PARTS
        rows = EXP_TILE // ISSUE_PARTS
        for c in range(ISSUE_PARTS):
            cols = slice(c * part * LANES, (c + 1) * part * LANES)
            o_ref[:, cols] = jnp.dot(h, wdn_bf[:, cols], preferred_element_type=F32)
            for r in range(c * rows, (c + 1) * rows):
                _token_copy(fl_hbm, src_ref[(i + 1) * EXP_TILE + r], xbuf.at[1 - slot], r,
                          sems.at[1 - slot]).start()

    @pl.when(i == nt)
    def _():
        wait_gather(slot)

    @pl.when(i >= nt)
    def _():
        o_ref[...] = jnp.zeros_like(o_ref)


def _experts(fl, tile_expert, src_rows, n_tiles_used, w_up, w_down, n_tiles):
    d = w_up.shape[1]
    last = lambda i, nt: jnp.minimum(i, nt[0] - 1)
    grid_spec = pltpu.PrefetchScalarGridSpec(
        num_scalar_prefetch=3,
        grid=(n_tiles,),
        in_specs=[pl.BlockSpec(memory_space=pl.ANY),
                  pl.BlockSpec((1, d, 2 * D_EXPERT), lambda i, te, src, nt: (te[last(i, nt)], 0, 0)),
                  pl.BlockSpec((1, D_EXPERT, d), lambda i, te, src, nt: (te[last(i, nt)], 0, 0))],
        out_specs=pl.BlockSpec((EXP_TILE, d), lambda i, te, src, nt: (i, 0)),
        scratch_shapes=[pltpu.VMEM((2, EXP_TILE * ROW_PITCH, LANES), F32),
                        pltpu.SemaphoreType.DMA((2,)),
                        pltpu.VMEM((d, 2 * D_EXPERT), BF16),
                        pltpu.VMEM((D_EXPERT, d), BF16)])
    return pl.pallas_call(
        _expert_kernel,
        grid_spec=grid_spec,
        out_shape=jax.ShapeDtypeStruct((n_tiles * EXP_TILE, d), F32),
        compiler_params=_cparams(1),
        name="moe_experts",
    )(tile_expert, src_rows, n_tiles_used, fl, w_up, w_down)


def _combine_kernel(final, pos_ref, ys_hbm, x_ref, rec_ref, m_ref, gf_ref, o_ref, ybuf, sems):
    i = pl.program_id(0)
    n = pl.num_programs(0)
    slot = i % 2

    def start_gather(tile, slot_):
        def body(r, c):
            for k in range(2):
                _row_copy(ys_hbm, pos_ref[(tile * ROW_TILE + r) * 2 + k], ybuf.at[slot_, k], r,
                          sems.at[slot_]).start()
            return c
        lax.fori_loop(0, ROW_TILE, body, 0, unroll=DMA_UNROLL)

    def wait_gather(slot_):
        for r in range(2 * ROW_TILE):
            _row_copy(ys_hbm, 0, ybuf.at[slot_, r % 2], r // 2, sems.at[slot_]).wait()

    @pl.when(i == 0)
    def _():
        start_gather(0, 0)

    wait_gather(slot)
    rec = rec_ref[...]
    w1 = rec[:, R_W1:R_W1 + 1]
    w2 = rec[:, R_W2:R_W2 + 1]
    part = o_ref.shape[1] // ISSUE_PARTS
    toks = ROW_TILE // ISSUE_PARTS
    for c in range(ISSUE_PARTS):
        cols = slice(c * part, (c + 1) * part)
        y = w1 * ybuf[slot, 0, :, cols] + w2 * ybuf[slot, 1, :, cols]
        o_ref[:, cols] = x_ref[:, cols] + m_ref[5:6, cols] * y
        for r in range(c * toks, (c + 1) * toks):
            for k in range(2):
                _row_copy(ys_hbm, pos_ref[((i + 1) * ROW_TILE + r) * 2 + k], ybuf.at[1 - slot, k], r,
                          sems.at[1 - slot]).start()
    if final:
        x = o_ref[...]
        r = lax.rsqrt(jnp.mean(x * x, axis=-1, keepdims=True) + EPS)
        o_ref[...] = (x * r) * gf_ref[...]

    @pl.when(i == n - 1)
    def _():
        wait_gather(1 - slot)


def _combine(lay, pos, ys, x_all, rec, mods, gain_final, first_tile, final):
    d = x_all.shape[1]
    n_rows = rec.shape[0]
    off = first_tile * (MM_TILE // ROW_TILE)
    grid_spec = pltpu.PrefetchScalarGridSpec(
        num_scalar_prefetch=1,
        grid=(n_rows // ROW_TILE,),
        in_specs=[pl.BlockSpec(memory_space=pl.ANY),
                  pl.BlockSpec((ROW_TILE, d), lambda i, pos: (i + off, 0)),
                  pl.BlockSpec((ROW_TILE, LANES), lambda i, pos: (i, 0)),
                  pl.BlockSpec((None, N_MOD, d), lambda i, pos: (lay.mod_row_of_seq_tile(i + off), 0, 0)),
                  pl.BlockSpec((1, d), lambda i, pos: (0, 0))],
        out_specs=pl.BlockSpec((ROW_TILE, d), lambda i, pos: (i, 0)),
        scratch_shapes=[pltpu.VMEM((2, 2, ROW_TILE, d), F32),
                        pltpu.SemaphoreType.DMA((2,))])
    return pl.pallas_call(
        functools.partial(_combine_kernel, final),
        grid_spec=grid_spec,
        out_shape=jax.ShapeDtypeStruct((n_rows, d), F32),
        compiler_params=_cparams(1),
        name="moe_combine",
    )(jnp.concatenate([pos, jnp.zeros((2 * ROW_TILE,), jnp.int32)]), ys, x_all, rec, mods,
      gain_final.reshape(1, d))


def _dispatch_plan(rec, counts, n_tiles):
    n_rows = rec.shape[0]
    cnt = counts[0, :N_EXPERTS].astype(jnp.int32)
    tiles_per = (cnt + EXP_TILE - 1) // EXP_TILE
    tile_end = jnp.cumsum(tiles_per)
    offset = (tile_end - tiles_per) * EXP_TILE
    e_sel = rec[:, R_E1:R_E2 + 1].astype(jnp.int32)
    rank = rec[:, R_RANK1:R_RANK2 + 1].astype(jnp.int32)
    experts = jnp.arange(N_EXPERTS, dtype=jnp.int32)
    pos = (jnp.sum(jnp.where(e_sel[..., None] == experts, offset, 0), axis=-1) + rank).reshape(-1)
    token = jnp.repeat(jnp.arange(n_rows, dtype=jnp.int32), 2)
    src_rows = jnp.zeros((n_tiles * EXP_TILE,), jnp.int32).at[pos].set(token, unique_indices=True)
    n_used = tile_end[-1]
    tile_ids = jnp.minimum(jnp.arange(n_tiles, dtype=jnp.int32), n_used - 1)
    tile_expert = jnp.sum(tile_end[None, :] <= tile_ids[:, None], axis=1).astype(jnp.int32)
    return pos, src_rows, tile_expert, n_used.reshape(1).astype(jnp.int32)


def _moe(lay, x_all, gain, mods, w_grp, b_grp, w_exp, b_exp, w_up_all, w_down_all, layer, gain_final,
         first_tile, final):
    fl, rec, counts = _router(lay, x_all, gain, mods, w_exp, b_exp, w_grp, b_grp, first_tile)
    n_rows = rec.shape[0]
    assert (2 * n_rows) % EXP_TILE == 0
    n_tiles = (2 * n_rows) // EXP_TILE + N_EXPERTS
    pos, src_rows, tile_expert, n_used = _dispatch_plan(rec, counts, n_tiles)
    d = x_all.shape[1]
    w_up = w_up_all.reshape(-1, d, 2 * D_EXPERT)
    w_down = w_down_all.reshape(-1, D_EXPERT, d)
    ys = _experts(fl, tile_expert + layer * N_EXPERTS, src_rows, n_used, w_up, w_down, n_tiles)
    return _combine(lay, pos, ys, x_all, rec, mods, gain_final, first_tile, final)


def kernel(x, c, ctx, c_ctx, mod_w, mod_b, norm_mix, norm_ffn, w_in, w_out, rg_conv_w, rg_conv_b,
           rg_gate_w, rg_gate_b, rg_lam, sc_conv_w, pool_w, pool_b, pool_scale, da_lam, da_norm,
           moe_grp_w, moe_grp_b, moe_exp_w, moe_exp_b, moe_up, moe_down, norm_final):
    n_batch, seq, d = x.shape
    n_ctx = ctx.shape[1]
    depth = mod_w.shape[0]
    lay = _Layout(n_batch, n_ctx, seq)
    assert 1 + n_batch <= SUBLANES

    cc = jnp.zeros((SUBLANES, d), F32).at[0].set(c_ctx).at[1:1 + n_batch].set(c)
    mods_all = _modulation(cc, mod_w, mod_b)
    cos, sin = _rope_tables(lay)
    x_all = (ctx.reshape(n_batch * n_ctx, d), x.reshape(n_batch * seq, d))

    for l in range(depth):
        last = l == depth - 1
        lam_init = 0.8 - 0.6 * math.exp(-0.3 * l)
        mods = mods_all[l]

        proj = _in_projection(lay, x_all, norm_mix[l], mods, w_in[l].astype(BF16))
        hf = _rglru(lay, proj, rg_conv_w[l], rg_conv_b[l], rg_gate_w[l, 0], rg_gate_b[l, 0],
                    rg_lam[l, 0], reverse=False)
        ya = _rglru(lay, proj, rg_conv_w[l], rg_conv_b[l], rg_gate_w[l, 1], rg_gate_b[l, 1],
                    rg_lam[l, 1], reverse=True, hf=hf)
        yb, yc = _conv_pool(lay, proj, sc_conv_w[l], pool_w[l], pool_b[l], pool_scale[l])
        qs, kr, vt = _rope(lay, proj, cos, sin)
        yd = _attention(lay, qs, kr, vt, da_lam[l], da_norm[l], lam_init)
        x_all = _out_projection(lay, (ya, yb, yc, yd), w_out[l].astype(BF16), x_all, mods)

        first_tile = 1 if last else 0
        out = _moe(lay, x_all, norm_ffn[l], mods, moe_grp_w[l], moe_grp_b[l], moe_exp_w[l], moe_exp_b[l],
                   moe_up, moe_down, l, norm_final, first_tile, last)
        if last:
            return out.reshape(n_batch, seq, d)
        x_all = out
```
